```python
import math
import jax, jax.numpy as jnp
from jax import lax
import numpy as np

D_MODEL = 2048
BATCH = 4
SEQ = 8192
DEPTH = 1

D_MIX = D_MODEL
N_ATTN_HEADS = 8
DIFF_HEAD_DIM = 64
ATTN_WIDTH = N_ATTN_HEADS * 2 * DIFF_HEAD_DIM
CONV_WIDTH = D_MIX - ATTN_WIDTH
CONV_TAPS = 3
IN_COLS = 3 * ATTN_WIDTH + 3 * CONV_WIDTH
Q_BLOCK = 128
N_MEM = 256
MEM_HEADS = 4
MEM_HEAD_DIM = D_MODEL // MEM_HEADS
N_EXPERTS = 32
TOP_K = 4
D_FF = D_MODEL
SWIGLU_LIMIT = 7.0
SWIGLU_ALPHA = 1.702
EXPERT_BLOCK = 512
DEEPNORM_ALPHA = (2.0 * DEPTH) ** 0.25
DEEPNORM_BETA = (8.0 * DEPTH) ** -0.25
LN_EPS = 1e-5

kernel_name = "hybrid_diffattn_shortconv_memxattn_moe_deepnorm"


def _layer_norm(x, g, b):
    xf = x.astype(jnp.float32)
    mu = jnp.mean(xf, axis=-1, keepdims=True)
    var = jnp.mean(jnp.square(xf - mu), axis=-1, keepdims=True)
    y = (xf - mu) * lax.rsqrt(var + LN_EPS)
    return (y * g.astype(jnp.float32) + b.astype(jnp.float32)).astype(x.dtype)


def _rms_norm(x, g):
    xf = x.astype(jnp.float32)
    y = xf * lax.rsqrt(jnp.mean(jnp.square(xf), axis=-1, keepdims=True) + LN_EPS)
    return (y * g.astype(jnp.float32)).astype(x.dtype)


def _alibi_slopes(n_heads):
    return jnp.exp2(-8.0 / n_heads * jnp.arange(1, n_heads + 1, dtype=jnp.float32))


def _diff_attention(q, k, v, lam):
    bsz, seq = q.shape[0], q.shape[1]
    nqb = seq // Q_BLOCK
    qb = q.reshape(bsz, nqb, Q_BLOCK, N_ATTN_HEADS, 2, DIFF_HEAD_DIM).transpose(1, 0, 4, 3, 2, 5)
    kt = k.transpose(0, 3, 2, 1, 4)
    vt = v.transpose(0, 2, 1, 3)
    slopes = _alibi_slopes(N_ATTN_HEADS)
    kpos = jnp.arange(seq, dtype=jnp.int32)
    scale = DIFF_HEAD_DIM ** -0.5

    def block(args):
        q_blk, i = args
        qpos = i * Q_BLOCK + jnp.arange(Q_BLOCK, dtype=jnp.int32)
        dist = jnp.abs(qpos[:, None] - kpos[None, :]).astype(jnp.float32)
        bias = -slopes[:, None, None] * dist
        s = jnp.einsum('bmhqd,bmhkd->bmhqk', q_blk, kt).astype(jnp.float32) * scale + bias
        p = jax.nn.softmax(s, axis=-1)
        a = (p[:, 0] - lam * p[:, 1]).astype(vt.dtype)
        return jnp.einsum('bhqk,bhkd->bhqd', a, vt)

    o = lax.map(block, (qb, jnp.arange(nqb, dtype=jnp.int32)))
    return o.transpose(1, 0, 3, 2, 4).reshape(bsz, seq, N_ATTN_HEADS, 2 * DIFF_HEAD_DIM)


def _centred_depthwise_conv3(u, w):
    up = jnp.pad(u, ((0, 0), (1, 1), (0, 0)))
    return up[:, :-2] * w[0] + up[:, 1:-1] * w[1] + up[:, 2:] * w[2]


def _hybrid_mixer(h, w_in, conv_w, subln_w, lq1, lk1, lq2, lk2, w_out, lambda_init):
    bsz, seq, _ = h.shape
    proj = h @ w_in
    a, c = ATTN_WIDTH, CONV_WIDTH
    q, k, v, gate_b, gate_c, u = jnp.split(
        proj, [a, 2 * a, 3 * a, 3 * a + c, 3 * a + 2 * c], axis=-1)
    q = q.reshape(bsz, seq, N_ATTN_HEADS, 2, DIFF_HEAD_DIM)
    k = k.reshape(bsz, seq, N_ATTN_HEADS, 2, DIFF_HEAD_DIM)
    v = v.reshape(bsz, seq, N_ATTN_HEADS, 2 * DIFF_HEAD_DIM)
    lam = (jnp.exp(jnp.sum(lq1.astype(jnp.float32) * lk1.astype(jnp.float32)))
           - jnp.exp(jnp.sum(lq2.astype(jnp.float32) * lk2.astype(jnp.float32)))
           + lambda_init)
    o_attn = _diff_attention(q, k, v, lam)
    o_attn = (_rms_norm(o_attn, subln_w) * (1.0 - lambda_init)).reshape(bsz, seq, ATTN_WIDTH)
    o_conv = gate_b * _centred_depthwise_conv3(gate_c * u, conv_w)
    return jnp.concatenate([o_attn, o_conv], axis=-1) @ w_out


def _memory_attention(h, mem, wq, wkv, wo):
    bsz, seq, _ = h.shape
    n_mem = mem.shape[1]
    q = (h @ wq).reshape(bsz, seq, MEM_HEADS, MEM_HEAD_DIM)
    k, v = jnp.split(mem @ wkv, 2, axis=-1)
    k = k.reshape(bsz, n_mem, MEM_HEADS, MEM_HEAD_DIM)
    v = v.reshape(bsz, n_mem, MEM_HEADS, MEM_HEAD_DIM)
    s = jnp.einsum('bshd,bmhd->bhsm', q, k).astype(jnp.float32) * (MEM_HEAD_DIM ** -0.5)
    p = jax.nn.softmax(s, axis=-1).astype(v.dtype)
    o = jnp.einsum('bhsm,bmhd->bshd', p, v).reshape(bsz, seq, D_MODEL)
    return o @ wo


def _moe(h, router_w, router_b, w_gate, b_gate, w_up, b_up, w_down, b_down):
    bsz, seq, d = h.shape
    n_tok = bsz * seq
    xf = h.reshape(n_tok, d)
    logits = (xf @ router_w + router_b).astype(jnp.float32)
    top_v, top_e = lax.top_k(logits, TOP_K)
    gates = jax.nn.softmax(top_v, axis=-1).astype(h.dtype)
    n_assign = n_tok * TOP_K
    flat_e = top_e.reshape(n_assign).astype(jnp.int32)
    flat_tok = jnp.repeat(jnp.arange(n_tok, dtype=jnp.int32), TOP_K)
    flat_g = gates.reshape(n_assign)
    order = jnp.argsort(flat_e)
    se, stok, sg = flat_e[order], flat_tok[order], flat_g[order]
    counts = jnp.bincount(flat_e, length=N_EXPERTS).astype(jnp.int32)
    starts = jnp.cumsum(counts) - counts
    padded = (counts + EXPERT_BLOCK - 1) // EXPERT_BLOCK * EXPERT_BLOCK
    pad_ends = jnp.cumsum(padded)
    pad_starts = pad_ends - padded
    dest = pad_starts[se] + (jnp.arange(n_assign, dtype=jnp.int32) - starts[se])
    n_blocks = -(-n_assign // EXPERT_BLOCK) + N_EXPERTS
    n_rows = n_blocks * EXPERT_BLOCK
    row_tok = jnp.zeros((n_rows,), jnp.int32).at[dest].set(stok)
    row_g = jnp.zeros((n_rows,), h.dtype).at[dest].set(sg)
    block_start = jnp.arange(n_blocks, dtype=jnp.int32) * EXPERT_BLOCK
    block_e = jnp.minimum(jnp.searchsorted(pad_ends, block_start, side='right'),
                          N_EXPERTS - 1).astype(jnp.int32)

    def expert_block(args):
        e, tok, g = args
        xb = xf[tok]
        gate = xb @ w_gate[e] + b_gate[e]
        up = xb @ w_up[e] + b_up[e]
        gate = jnp.minimum(gate, SWIGLU_LIMIT)
        up = jnp.clip(up, -SWIGLU_LIMIT, SWIGLU_LIMIT)
        act = gate * jax.nn.sigmoid(SWIGLU_ALPHA * gate) * (up + 1.0)
        return (act @ w_down[e] + b_down[e]) * g[:, None]

    outs = lax.map(expert_block, (block_e,
                                  row_tok.reshape(n_blocks, EXPERT_BLOCK),
                                  row_g.reshape(n_blocks, EXPERT_BLOCK)))
    y = jnp.zeros((n_tok, d), h.dtype).at[row_tok].add(outs.reshape(n_rows, d))
    return y.reshape(bsz, seq, d)


def _nrm(k, shape, scale):
    return scale * jax.random.normal(k, shape, jnp.float32)


def setup_inputs(seed: int = 0) -> dict:
    key = jax.random.key(seed)
    ks = jax.random.split(key, 32)
    L, D, E, F = DEPTH, D_MODEL, N_EXPERTS, D_FF
    beta = DEEPNORM_BETA
    col_scale = jnp.concatenate([
        jnp.ones((2 * ATTN_WIDTH,), jnp.float32),
        jnp.full((ATTN_WIDTH,), beta, jnp.float32),
        jnp.ones((3 * CONV_WIDTH,), jnp.float32)])
    kv_scale = jnp.concatenate([jnp.ones((D,), jnp.float32), jnp.full((D,), beta, jnp.float32)])
    return {
        "x": _nrm(ks[0], (BATCH, SEQ, D), 1.0),
        "mem": _nrm(ks[1], (BATCH, N_MEM, D), 1.0),
        "w_in": _nrm(ks[2], (L, D, IN_COLS), D ** -0.5) * col_scale,
        "conv_w": _nrm(ks[3], (L, CONV_TAPS, CONV_WIDTH), 0.5),
        "attn_subln_w": 1.0 + _nrm(ks[4], (L, 2 * DIFF_HEAD_DIM), 0.02),
        "lambda_q1": _nrm(ks[5], (L, DIFF_HEAD_DIM), 0.1),
        "lambda_k1": _nrm(ks[6], (L, DIFF_HEAD_DIM), 0.1),
        "lambda_q2": _nrm(ks[7], (L, DIFF_HEAD_DIM), 0.1),
        "lambda_k2": _nrm(ks[8], (L, DIFF_HEAD_DIM), 0.1),
        "w_out": _nrm(ks[9], (L, D_MIX, D), D_MIX ** -0.5 * beta),
        "ln1_g": 1.0 + _nrm(ks[10], (L, D), 0.02),
        "ln1_b": _nrm(ks[11], (L, D), 0.02),
        "mem_wq": _nrm(ks[12], (L, D, D), D ** -0.5),
        "mem_wkv": _nrm(ks[13], (L, D, 2 * D), D ** -0.5) * kv_scale,
        "mem_wo": _nrm(ks[14], (L, D, D), D ** -0.5 * beta),
        "ln2_g": 1.0 + _nrm(ks[15], (L, D), 0.02),
        "ln2_b": _nrm(ks[16], (L, D), 0.02),
        "router_w": _nrm(ks[17], (L, D, E), D ** -0.5),
        "router_b": _nrm(ks[18], (L, E), 0.01),
        "w_gate": _nrm(ks[19], (L, E, D, F), D ** -0.5),
        "b_gate": _nrm(ks[20], (L, E, F), 0.01),
        "w_up": _nrm(ks[21], (L, E, D, F), D ** -0.5),
        "b_up": _nrm(ks[22], (L, E, F), 0.01),
        "w_down": _nrm(ks[23], (L, E, F, D), F ** -0.5 * beta),
        "b_down": _nrm(ks[24], (L, E, D), 0.01),
        "ln3_g": 1.0 + _nrm(ks[25], (L, D), 0.02),
        "ln3_b": _nrm(ks[26], (L, D), 0.02),
    }


def reference(x, mem, w_in, conv_w, attn_subln_w, lambda_q1, lambda_k1, lambda_q2, lambda_k2,
              w_out, ln1_g, ln1_b, mem_wq, mem_wkv, mem_wo, ln2_g, ln2_b,
              router_w, router_b, w_gate, b_gate, w_up, b_up, w_down, b_down,
              ln3_g, ln3_b):
    for l in range(DEPTH):
        lambda_init = 0.8 - 0.6 * math.exp(-0.3 * l)
        mix = _hybrid_mixer(x, w_in[l], conv_w[l], attn_subln_w[l], lambda_q1[l], lambda_k1[l],
                            lambda_q2[l], lambda_k2[l], w_out[l], lambda_init)
        x = _layer_norm(DEEPNORM_ALPHA * x + mix, ln1_g[l], ln1_b[l])
        xa = _memory_attention(x, mem, mem_wq[l], mem_wkv[l], mem_wo[l])
        x = _layer_norm(DEEPNORM_ALPHA * x + xa, ln2_g[l], ln2_b[l])
        ff = _moe(x, router_w[l], router_b[l], w_gate[l], b_gate[l], w_up[l], b_up[l],
                  w_down[l], b_down[l])
        x = _layer_norm(DEEPNORM_ALPHA * x + ff, ln3_g[l], ln3_b[l])
    return x
```

```python
import functools
import math

import jax
import jax.numpy as jnp
from jax import lax
from jax.experimental import pallas as pl
from jax.experimental.pallas import tpu as pltpu

N_ATTN_HEADS = 8
DIFF_HEAD_DIM = 64
HEAD_WIDTH = 2 * DIFF_HEAD_DIM
MEM_HEADS = 4
TOP_K = 4
CONV_TAPS = 3
SWIGLU_LIMIT = 7.0
SWIGLU_ALPHA = 1.702
LN_EPS = 1e-5

LANES = 128
SUBLANES = 8
VMEM_LIMIT = 56 * 1024 * 1024
NEG_BIG = -1e30

F32 = jnp.float32
BF16 = jnp.bfloat16


def _cparams(sem, vmem=VMEM_LIMIT):
    return pltpu.CompilerParams(dimension_semantics=sem, vmem_limit_bytes=vmem)


def _resident(shape, index_map):
    return pl.BlockSpec(shape, index_map, pipeline_mode=pl.Buffered(1))


def _layer_norm(z, g, b):
    mu = jnp.mean(z, axis=-1, keepdims=True)
    d = z - mu
    var = jnp.mean(d * d, axis=-1, keepdims=True)
    return d * lax.rsqrt(var + LN_EPS) * g + b


def _dot(a, b):
    return jnp.dot(a, b, preferred_element_type=F32)


def _dot_nt(a, b):
    return lax.dot_general(a, b, (((1,), (1,)), ((), ())), preferred_element_type=F32)


def _proj_kernel(x_ref, w_ref, o_ref, xb_ref):
    @pl.when(pl.program_id(1) == 0)
    def _():
        xb_ref[...] = x_ref[...].astype(BF16)

    o_ref[...] = _dot(xb_ref[...], w_ref[...]).astype(o_ref.dtype)


def _projection(x, w, tm, tn):
    m, k = x.shape
    n = w.shape[1]
    tm, tn = min(tm, m), min(tn, n)
    return pl.pallas_call(
        _proj_kernel,
        grid=(m // tm, n // tn),
        in_specs=[pl.BlockSpec((tm, k), lambda i, j: (i, 0)),
                  pl.BlockSpec((k, tn), lambda i, j: (0, j))],
        out_specs=pl.BlockSpec((tm, tn), lambda i, j: (i, j)),
        out_shape=jax.ShapeDtypeStruct((m, n), BF16),
        scratch_shapes=[pltpu.VMEM((tm, k), BF16)],
        compiler_params=_cparams(("parallel", "arbitrary")),
    )(x, w)


def _attn_kernel(slopes_ref, q_ref, k_ref, v_ref, lq1_ref, lk1_ref, lq2_ref, lk2_ref, sub_ref,
                 o_ref, m1_ref, m2_ref, acc1_ref, acc2_ref, *, tq, tk, seq, lambda_init):
    h = pl.program_id(1)
    qi = pl.program_id(2)
    slope = slopes_ref[h]

    qs = q_ref[0] * jnp.asarray(DIFF_HEAD_DIM ** -0.5, BF16)
    lane = lax.broadcasted_iota(jnp.int32, (tq, HEAD_WIDTH), 1)
    zero = jnp.zeros_like(qs)
    q1 = jnp.where(lane < DIFF_HEAD_DIM, qs, zero)
    q2 = jnp.where(lane >= DIFF_HEAD_DIM, qs, zero)
    rel = (lax.broadcasted_iota(jnp.int32, (tq, tk), 0)
           - lax.broadcasted_iota(jnp.int32, (tq, tk), 1)).astype(F32)

    m1_ref[...] = jnp.full(m1_ref.shape, NEG_BIG, F32)
    m2_ref[...] = jnp.full(m2_ref.shape, NEG_BIG, F32)
    acc1_ref[...] = jnp.zeros(acc1_ref.shape, F32)
    acc2_ref[...] = jnp.zeros(acc2_ref.shape, F32)
    ones = jnp.ones((tk, HEAD_WIDTH), BF16)

    def update(qm, kblk, v_aug, bias, m_ref, acc_ref):
        s = _dot_nt(qm, kblk) + bias
        m_prev = m_ref[...]
        m_new = jnp.maximum(m_prev, jnp.max(s, axis=1, keepdims=True))
        p = jnp.exp(s - m_new)
        acc_ref[...] = jnp.exp(m_prev - m_new) * acc_ref[...] + _dot(p.astype(BF16), v_aug)
        m_ref[...] = m_new

    def body(ki, carry):
        start = pl.multiple_of(ki * tk, tk)
        kblk = k_ref[0, pl.ds(start, tk), :]
        v_aug = jnp.concatenate([v_ref[0, pl.ds(start, tk), :], ones], axis=1)
        off = (qi * tq - ki * tk).astype(F32)
        bias = -slope * jnp.abs(rel + off)
        update(q1, kblk, v_aug, bias, m1_ref, acc1_ref)
        update(q2, kblk, v_aug, bias, m2_ref, acc2_ref)
        return carry

    lax.fori_loop(0, seq // tk, body, 0)

    lam = (jnp.exp(jnp.sum(lq1_ref[...] * lk1_ref[...], keepdims=True))
           - jnp.exp(jnp.sum(lq2_ref[...] * lk2_ref[...], keepdims=True)) + lambda_init)
    acc1 = acc1_ref[...]
    acc2 = acc2_ref[...]
    o = (acc1[:, :HEAD_WIDTH] / acc1[:, HEAD_WIDTH:]
         - lam * (acc2[:, :HEAD_WIDTH] / acc2[:, HEAD_WIDTH:]))
    ms = jnp.mean(o * o, axis=-1, keepdims=True)
    y = o * lax.rsqrt(ms + LN_EPS) * sub_ref[...] * (1.0 - lambda_init)
    o_ref[0] = y.astype(o_ref.dtype)


def _diff_attention(proj3, slopes, lq1, lk1, lq2, lk2, subln, lambda_init, tq=512, tk=512):
    bsz, seq, _ = proj3.shape
    nh = N_ATTN_HEADS
    tq, tk = min(tq, seq), min(tk, seq)
    vec = lambda a: a.reshape(1, -1).astype(F32)
    small = lambda n: pl.BlockSpec((1, n), lambda b, h, i, s: (0, 0))
    grid_spec = pltpu.PrefetchScalarGridSpec(
        num_scalar_prefetch=1,
        grid=(bsz, nh, seq // tq),
        in_specs=[
            pl.BlockSpec((1, tq, HEAD_WIDTH), lambda b, h, i, s: (b, i, h)),
            pl.BlockSpec((1, seq, HEAD_WIDTH), lambda b, h, i, s: (b, 0, nh + h)),
            pl.BlockSpec((1, seq, HEAD_WIDTH), lambda b, h, i, s: (b, 0, 2 * nh + h)),
            small(DIFF_HEAD_DIM), small(DIFF_HEAD_DIM), small(DIFF_HEAD_DIM), small(DIFF_HEAD_DIM),
            small(HEAD_WIDTH),
        ],
        out_specs=pl.BlockSpec((1, tq, HEAD_WIDTH), lambda b, h, i, s: (b, i, h)),
        scratch_shapes=[pltpu.VMEM((tq, 1), F32), pltpu.VMEM((tq, 1), F32),
                        pltpu.VMEM((tq, 2 * HEAD_WIDTH), F32), pltpu.VMEM((tq, 2 * HEAD_WIDTH), F32)],
    )
    return pl.pallas_call(
        functools.partial(_attn_kernel, tq=tq, tk=tk, seq=seq, lambda_init=lambda_init),
        grid_spec=grid_spec,
        out_shape=jax.ShapeDtypeStruct((bsz, seq, nh * HEAD_WIDTH), BF16),
        compiler_params=_cparams(("parallel", "parallel", "parallel")),
    )(slopes, proj3, proj3, proj3, vec(lq1), vec(lk1), vec(lq2), vec(lk2), vec(subln))


def _outproj_kernel(oa_ref, gb_ref, gc_ref, u_ref, cp_ref, up_ref, cn_ref, un_ref, cw_ref,
                    wa_ref, wc_ref, x_ref, g_ref, b_ref, x1_ref, x1b_ref, *, tm, seq, alpha):
    i = pl.program_id(0)
    cu = gc_ref[...].astype(F32) * u_ref[...].astype(F32)
    row = lax.broadcasted_iota(jnp.int32, cu.shape, 0)
    last = SUBLANES - 1
    prev_edge = cp_ref[last:last + 1, :].astype(F32) * up_ref[last:last + 1, :].astype(F32)
    next_edge = cn_ref[0:1, :].astype(F32) * un_ref[0:1, :].astype(F32)
    prev_edge = jnp.where((i * tm) % seq == 0, 0.0, prev_edge)
    next_edge = jnp.where(((i + 1) * tm) % seq == 0, 0.0, next_edge)
    cu_prev = jnp.where(row == 0, prev_edge, pltpu.roll(cu, 1, 0))
    cu_next = jnp.where(row == tm - 1, next_edge, pltpu.roll(cu, tm - 1, 0))
    cw = cw_ref[...]
    conv = cu_prev * cw[0:1, :] + cu * cw[1:2, :] + cu_next * cw[2:3, :]
    oc = (gb_ref[...].astype(F32) * conv).astype(BF16)
    y = _dot(oa_ref[...], wa_ref[...]) + _dot(oc, wc_ref[...])
    x1 = _layer_norm(alpha * x_ref[...] + y, g_ref[...], b_ref[...])
    x1_ref[...] = x1
    x1b_ref[...] = x1.astype(BF16)


def _out_projection(oattn, proj, x, conv_w, w_out, ln_g, ln_b, seq, alpha, tm=512):
    t, d = x.shape
    aw = oattn.shape[1]
    cwid = d - aw
    assert aw == cwid and proj.shape[1] == 3 * aw + 3 * cwid
    tm = min(tm, seq)
    hb = tm // SUBLANES
    nhb = t // SUBLANES
    cb = 3 * aw // cwid
    prev_map = lambda c: (lambda i: (jnp.maximum(i * hb - 1, 0), c))
    next_map = lambda c: (lambda i: (jnp.minimum((i + 1) * hb, nhb - 1), c))
    wa = w_out[:aw].astype(BF16)
    wc = w_out[aw:].astype(BF16)
    row = lambda a: a.reshape(1, -1).astype(F32)
    return pl.pallas_call(
        functools.partial(_outproj_kernel, tm=tm, seq=seq, alpha=alpha),
        grid=(t // tm,),
        in_specs=[
            pl.BlockSpec((tm, aw), lambda i: (i, 0)),
            pl.BlockSpec((tm, cwid), lambda i: (i, cb)),
            pl.BlockSpec((tm, cwid), lambda i: (i, cb + 1)),
            pl.BlockSpec((tm, cwid), lambda i: (i, cb + 2)),
            pl.BlockSpec((SUBLANES, cwid), prev_map(cb + 1)),
            pl.BlockSpec((SUBLANES, cwid), prev_map(cb + 2)),
            pl.BlockSpec((SUBLANES, cwid), next_map(cb + 1)),
            pl.BlockSpec((SUBLANES, cwid), next_map(cb + 2)),
            _resident((CONV_TAPS, cwid), lambda i: (0, 0)),
            _resident((aw, d), lambda i: (0, 0)),
            _resident((cwid, d), lambda i: (0, 0)),
            pl.BlockSpec((tm, d), lambda i: (i, 0)),
            _resident((1, d), lambda i: (0, 0)),
            _resident((1, d), lambda i: (0, 0)),
        ],
        out_specs=[pl.BlockSpec((tm, d), lambda i: (i, 0)),
                   pl.BlockSpec((tm, d), lambda i: (i, 0))],
        out_shape=[jax.ShapeDtypeStruct((t, d), F32), jax.ShapeDtypeStruct((t, d), BF16)],
        compiler_params=_cparams(("parallel",)),
    )(oattn, proj, proj, proj, proj, proj, proj, proj, conv_w.astype(F32), wa, wc, x,
      row(ln_g), row(ln_b))


def _memattn_kernel(x1b_ref, x1_ref, wq_ref, kv_ref, wo_ref, g_ref, b_ref, rwh_ref, rwl_ref, rb_ref,
                    x2_ref, x2p_ref, te_ref, tg_ref, rk_ref, cnt_ref, carry_ref, *, tm, alpha):
    i = pl.program_id(0)
    d = x1_ref.shape[1]
    hd = d // MEM_HEADS

    @pl.when(i == 0)
    def _():
        carry_ref[...] = jnp.zeros(carry_ref.shape, F32)

    q = _dot(x1b_ref[...], wq_ref[...]).astype(BF16)
    scale = hd ** -0.5
    heads = []
    for hh in range(MEM_HEADS):
        qh = q[:, hh * hd:(hh + 1) * hd]
        kh = kv_ref[:, hh * hd:(hh + 1) * hd]
        vh = kv_ref[:, d + hh * hd:d + (hh + 1) * hd]
        s = _dot_nt(qh, kh) * scale
        p = jnp.exp(s - jnp.max(s, axis=1, keepdims=True))
        p = p / jnp.sum(p, axis=1, keepdims=True)
        heads.append(_dot(p.astype(BF16), vh).astype(BF16))
    xa = _dot(jnp.concatenate(heads, axis=1), wo_ref[...])
    x2 = _layer_norm(alpha * x1_ref[...] + xa, g_ref[...], b_ref[...])
    x2_ref[...] = x2

    half = d // 2
    lo = lax.bitcast_convert_type(x2[:, :half].astype(BF16).astype(F32), jnp.uint32)
    hi = lax.bitcast_convert_type(x2[:, half:].astype(BF16).astype(F32), jnp.uint32)
    x2p_ref[...] = (lo >> 16) | (hi & jnp.uint32(0xFFFF0000))

    xh = x2.astype(BF16)
    xl = (x2 - xh.astype(F32)).astype(BF16)
    logits = (_dot(xh, rwh_ref[...]) + _dot(xh, rwl_ref[...]) + _dot(xl, rwh_ref[...])
              + rb_ref[...])

    lane = lax.broadcasted_iota(jnp.int32, logits.shape, 1)
    work = logits
    vals, sels, idxs = [], [], []
    for _ in range(TOP_K):
        mk = jnp.max(work, axis=1, keepdims=True)
        idx = jnp.min(jnp.where(work == mk, lane, LANES), axis=1, keepdims=True)
        sel = lane == idx
        vals.append(mk)
        idxs.append(idx)
        sels.append(sel)
        work = jnp.where(sel, -jnp.inf, work)
    exps = [jnp.exp(v - vals[0]) for v in vals]
    denom = exps[0]
    for e in exps[1:]:
        denom = denom + e

    onehot = jnp.zeros(logits.shape, F32)
    for sel in sels:
        onehot = onehot + sel.astype(F32)
    lower = (lax.broadcasted_iota(jnp.int32, (tm, tm), 0)
             > lax.broadcasted_iota(jnp.int32, (tm, tm), 1)).astype(BF16)
    before = _dot(lower, onehot.astype(BF16)) + carry_ref[...]

    te = jnp.zeros(logits.shape, jnp.int32)
    tg = jnp.zeros(logits.shape, F32)
    rk = jnp.zeros(logits.shape, jnp.int32)
    for k in range(TOP_K):
        rank_k = jnp.sum(jnp.where(sels[k], before, 0.0), axis=1, keepdims=True)
        te = jnp.where(lane == k, idxs[k], te)
        tg = jnp.where(lane == k, exps[k] / denom, tg)
        rk = jnp.where(lane == k, rank_k.astype(jnp.int32), rk)
    te_ref[...] = te
    tg_ref[...] = tg
    rk_ref[...] = rk
    carry_ref[...] = carry_ref[...] + jnp.sum(onehot, axis=0, keepdims=True)
    cnt_ref[...] = carry_ref[...]


def _memory_attention(x1b, x1, kv, wq, wo, ln_g, ln_b, router_w, router_b, seq, n_mem, alpha, tm=256):
    t, d = x1.shape
    tm = min(tm, seq)
    ne = router_w.shape[1]
    assert ne <= LANES
    rw = jnp.zeros((d, LANES), F32).at[:, :ne].set(router_w.astype(F32))
    rwh = rw.astype(BF16)
    rwl = (rw - rwh.astype(F32)).astype(BF16)
    rb = jnp.full((1, LANES), NEG_BIG, F32).at[0, :ne].set(router_b.astype(F32))
    row = lambda a: a.reshape(1, -1).astype(F32)
    tiles_per_seq = seq // tm
    tile_out = lambda w, dt: (pl.BlockSpec((tm, w), lambda i: (i, 0)), jax.ShapeDtypeStruct((t, w), dt))
    outs = [tile_out(d, F32), tile_out(d // 2, jnp.uint32), tile_out(LANES, jnp.int32),
            tile_out(LANES, F32), tile_out(LANES, jnp.int32),
            (pl.BlockSpec((1, LANES), lambda i: (0, 0)), jax.ShapeDtypeStruct((1, LANES), F32))]
    return pl.pallas_call(
        functools.partial(_memattn_kernel, tm=tm, alpha=alpha),
        grid=(t // tm,),
        in_specs=[
            pl.BlockSpec((tm, d), lambda i: (i, 0)),
            pl.BlockSpec((tm, d), lambda i: (i, 0)),
            _resident((d, d), lambda i: (0, 0)),
            pl.BlockSpec((n_mem, 2 * d), lambda i: (i // tiles_per_seq, 0)),
            _resident((d, d), lambda i: (0, 0)),
            _resident((1, d), lambda i: (0, 0)),
            _resident((1, d), lambda i: (0, 0)),
            _resident((d, LANES), lambda i: (0, 0)),
            _resident((d, LANES), lambda i: (0, 0)),
            _resident((1, LANES), lambda i: (0, 0)),
        ],
        out_specs=[o[0] for o in outs],
        out_shape=[o[1] for o in outs],
        scratch_shapes=[pltpu.VMEM((1, LANES), F32)],
        compiler_params=_cparams(("arbitrary",)),
    )(x1b, x1, wq.astype(BF16), kv, wo.astype(BF16), row(ln_g), row(ln_b), rwh, rwl, rb)


def _dispatch_kernel(dest_ref, x2p_ref, init_ref, xs_ref, sem, *, rows):
    del init_ref
    base = pl.program_id(0) * rows * TOP_K

    def issue(r, carry):
        for k in range(TOP_K):
            dst = dest_ref[base + r * TOP_K + k]
            pltpu.make_async_copy(x2p_ref.at[pl.ds(r, 1)], xs_ref.at[pl.ds(dst, 1)], sem).start()
        return carry

    lax.fori_loop(0, rows, issue, 0)
    for _ in range(TOP_K):
        pltpu.make_async_copy(x2p_ref, xs_ref.at[pl.ds(0, rows)], sem).wait()


def _dispatch(dest, x2p, n_rows, rows=512):
    t, w = x2p.shape
    rows = min(rows, t)
    init = jnp.zeros((n_rows, w), x2p.dtype)
    grid_spec = pltpu.PrefetchScalarGridSpec(
        num_scalar_prefetch=1,
        grid=(t // rows,),
        in_specs=[pl.BlockSpec((rows, w), lambda i, s: (i, 0)),
                  pl.BlockSpec(memory_space=pl.ANY)],
        out_specs=pl.BlockSpec(memory_space=pl.ANY),
        scratch_shapes=[pltpu.SemaphoreType.DMA],
    )
    return pl.pallas_call(
        functools.partial(_dispatch_kernel, rows=rows),
        grid_spec=grid_spec,
        out_shape=jax.ShapeDtypeStruct((n_rows, w), x2p.dtype),
        input_output_aliases={2: 0},
        compiler_params=_cparams(("arbitrary",)),
    )(dest, x2p, init)


def _unpack_rows(words):
    lo = lax.bitcast_convert_type(words << 16, F32).astype(BF16)
    hi = lax.bitcast_convert_type(words & jnp.uint32(0xFFFF0000), F32).astype(BF16)
    return lo, hi


def _expert_up_kernel(ie_ref, in_ref, irt_ref, ivalid_ref, ifirst_ref, xs_ref, wg_ref, wu_ref,
                      bg_ref, bu_ref, h_ref, wgb_ref, wub_ref):
    w = pl.program_id(0)

    @pl.when(ifirst_ref[w] == 1)
    def _():
        wgb_ref[...] = wg_ref[0].astype(BF16)
        wub_ref[...] = wu_ref[0].astype(BF16)

    @pl.when(ivalid_ref[w] == 1)
    def _():
        lo, hi = _unpack_rows(xs_ref[...])
        half = lo.shape[1]
        gate = _dot(lo, wgb_ref[:half, :]) + _dot(hi, wgb_ref[half:, :]) + bg_ref[0]
        up = _dot(lo, wub_ref[:half, :]) + _dot(hi, wub_ref[half:, :]) + bu_ref[0]
        gate = jnp.minimum(gate, SWIGLU_LIMIT)
        up = jnp.clip(up, -SWIGLU_LIMIT, SWIGLU_LIMIT)
        act = gate * jax.nn.sigmoid(SWIGLU_ALPHA * gate) * (up + 1.0)
        h_ref[...] = act.astype(h_ref.dtype)

    @pl.when(ivalid_ref[w] == 0)
    def _():
        h_ref[...] = jnp.zeros(h_ref.shape, h_ref.dtype)


def _expert_down_kernel(ie_ref, in_ref, irt_ref, ivalid_ref, ifirst_ref, h_ref, wd_ref, bd_ref,
                        o_ref, wdb_ref):
    w = pl.program_id(0)

    @pl.when(ifirst_ref[w] == 1)
    def _():
        wdb_ref[...] = wd_ref[0].astype(BF16)

    @pl.when(ivalid_ref[w] == 1)
    def _():
        o_ref[...] = _dot(h_ref[...], wdb_ref[...]) + bd_ref[0]

    @pl.when(ivalid_ref[w] == 0)
    def _():
        o_ref[...] = jnp.zeros(o_ref.shape, o_ref.dtype)


def _work_items(tiles_e, n_tiles_max, n_col):
    ne = tiles_e.shape[0]
    cum = jnp.cumsum(tiles_e)
    total_tiles = cum[-1]
    first_tile = cum - tiles_e
    item_end = cum * n_col
    w = jnp.arange(n_tiles_max * n_col, dtype=jnp.int32)
    e = jnp.minimum(jnp.searchsorted(item_end, w, side='right'), ne - 1).astype(jnp.int32)
    te = jnp.maximum(tiles_e[e], 1)
    local = w - (item_end[e] - tiles_e[e] * n_col)
    valid = w < total_tiles * n_col
    spare = jnp.maximum(n_tiles_max - total_tiles, 1)
    j = w - total_tiles * n_col
    col = jnp.where(valid, local // te, j // spare)
    rt = jnp.where(valid, first_tile[e] + local % te, total_tiles + j % spare)
    first = jnp.where(valid, (local % te) == 0, False)
    i32 = lambda a: a.astype(jnp.int32)
    return i32(e), i32(col), i32(rt), i32(valid), i32(first)


def _expert_up(items, xs, w_gate, w_up, b_gate, b_up, tm, tn):
    n_rows, half = xs.shape
    ne, d, f = w_gate.shape
    tn = min(tn, f)
    n_items = items[0].shape[0]
    wspec = pl.BlockSpec((1, d, tn), lambda w, ie, ic, irt, iv, ifi: (ie[w], 0, ic[w]))
    bspec = pl.BlockSpec((1, 1, tn), lambda w, ie, ic, irt, iv, ifi: (ie[w], 0, ic[w]))
    grid_spec = pltpu.PrefetchScalarGridSpec(
        num_scalar_prefetch=5,
        grid=(n_items,),
        in_specs=[pl.BlockSpec((tm, half), lambda w, ie, ic, irt, iv, ifi: (irt[w], 0)),
                  wspec, wspec, bspec, bspec],
        out_specs=pl.BlockSpec((tm, tn), lambda w, ie, ic, irt, iv, ifi: (irt[w], ic[w])),
        scratch_shapes=[pltpu.VMEM((d, tn), BF16), pltpu.VMEM((d, tn), BF16)],
    )
    return pl.pallas_call(
        _expert_up_kernel,
        grid_spec=grid_spec,
        out_shape=jax.ShapeDtypeStruct((n_rows, f), BF16),
        compiler_params=_cparams(("arbitrary",)),
    )(*items, xs, w_gate, w_up, b_gate.reshape(ne, 1, f), b_up.reshape(ne, 1, f))


def _expert_down(items, h, w_down, b_down, tm, tn):
    n_rows, f = h.shape
    ne, _, d = w_down.shape
    tn = min(tn, d)
    n_items = items[0].shape[0]
    grid_spec = pltpu.PrefetchScalarGridSpec(
        num_scalar_prefetch=5,
        grid=(n_items,),
        in_specs=[pl.BlockSpec((tm, f), lambda w, ie, ic, irt, iv, ifi: (irt[w], 0)),
                  pl.BlockSpec((1, f, tn), lambda w, ie, ic, irt, iv, ifi: (ie[w], 0, ic[w])),
                  pl.BlockSpec((1, 1, tn), lambda w, ie, ic, irt, iv, ifi: (ie[w], 0, ic[w]))],
        out_specs=pl.BlockSpec((tm, tn), lambda w, ie, ic, irt, iv, ifi: (irt[w], ic[w])),
        scratch_shapes=[pltpu.VMEM((f, tn), BF16)],
    )
    return pl.pallas_call(
        _expert_down_kernel,
        grid_spec=grid_spec,
        out_shape=jax.ShapeDtypeStruct((n_rows, d), F32),
        compiler_params=_cparams(("arbitrary",)),
    )(*items, h, w_down, b_down.reshape(ne, 1, d))


def _combine_kernel(dest_ref, os_ref, tg_ref, x2_ref, g_ref, b_ref, o_ref, buf_ref, sem, *, rows, alpha):
    base = pl.program_id(0) * rows * TOP_K

    def issue(r, carry):
        for k in range(TOP_K):
            src = dest_ref[base + r * TOP_K + k]
            pltpu.make_async_copy(os_ref.at[pl.ds(src, 1)], buf_ref.at[k, pl.ds(r, 1)], sem).start()
        return carry

    lax.fori_loop(0, rows, issue, 0)
    for k in range(TOP_K):
        pltpu.make_async_copy(os_ref.at[pl.ds(0, rows)], buf_ref.at[k], sem).wait()

    tg = tg_ref[...]
    y = tg[:, 0:1] * buf_ref[0]
    for k in range(1, TOP_K):
        y = y + tg[:, k:k + 1] * buf_ref[k]
    o_ref[...] = _layer_norm(alpha * x2_ref[...] + y, g_ref[...], b_ref[...])


def _combine(dest, out_sorted, tg, x2, ln_g, ln_b, alpha, rows=256):
    t, d = x2.shape
    rows = min(rows, t)
    row = lambda a: a.reshape(1, -1).astype(F32)
    grid_spec = pltpu.PrefetchScalarGridSpec(
        num_scalar_prefetch=1,
        grid=(t // rows,),
        in_specs=[pl.BlockSpec(memory_space=pl.ANY),
                  pl.BlockSpec((rows, LANES), lambda i, s: (i, 0)),
                  pl.BlockSpec((rows, d), lambda i, s: (i, 0)),
                  pl.BlockSpec((1, d), lambda i, s: (0, 0)),
                  pl.BlockSpec((1, d), lambda i, s: (0, 0))],
        out_specs=pl.BlockSpec((rows, d), lambda i, s: (i, 0)),
        scratch_shapes=[pltpu.VMEM((TOP_K, rows, d), F32), pltpu.SemaphoreType.DMA],
    )
    return pl.pallas_call(
        functools.partial(_combine_kernel, rows=rows, alpha=alpha),
        grid_spec=grid_spec,
        out_shape=jax.ShapeDtypeStruct((t, d), F32),
        compiler_params=_cparams(("arbitrary",)),
    )(dest, out_sorted, tg, x2, row(ln_g), row(ln_b))


def _moe(x2, x2p, te, tg, rk, counts, w_gate, b_gate, w_up, b_up, w_down, b_down, ln_g, ln_b,
         alpha, tm=512, tn_up=512, tn_down=1024):
    t, d = x2.shape
    ne = w_gate.shape[0]
    n_assign = t * TOP_K
    tm = min(tm, t)
    n_tiles_max = -(-n_assign // tm) + ne
    n_rows = n_tiles_max * tm

    counts = counts[0, :ne].astype(jnp.int32)
    tiles_e = (counts + tm - 1) // tm
    pad_start = (jnp.cumsum(tiles_e) - tiles_e) * tm
    dest = (pad_start[te[:, :TOP_K]] + rk[:, :TOP_K]).reshape(n_assign).astype(jnp.int32)

    xs = _dispatch(dest, x2p, n_rows)
    f = w_gate.shape[2]
    items_up = _work_items(tiles_e, n_tiles_max, f // min(tn_up, f))
    h = _expert_up(items_up, xs, w_gate, w_up, b_gate, b_up, tm, tn_up)
    items_dn = _work_items(tiles_e, n_tiles_max, d // min(tn_down, d))
    out_sorted = _expert_down(items_dn, h, w_down, b_down, tm, tn_down)
    return _combine(dest, out_sorted, tg, x2, ln_g, ln_b, alpha)


def kernel(x, mem, w_in, conv_w, attn_subln_w, lambda_q1, lambda_k1, lambda_q2, lambda_k2, w_out, ln1_g, ln1_b, mem_wq, mem_wkv, mem_wo, ln2_g, ln2_b, router_w, router_b, w_gate, b_gate, w_up, b_up, w_down, b_down, ln3_g, ln3_b):
    bsz, seq, d = x.shape
    n_mem = mem.shape[1]
    depth = w_in.shape[0]
    alpha = (2.0 * depth) ** 0.25
    slopes = jnp.exp2(-8.0 / N_ATTN_HEADS * jnp.arange(1, N_ATTN_HEADS + 1, dtype=F32))
    xf = x.reshape(bsz * seq, d)
    memf = mem.reshape(bsz * n_mem, d)
    for l in range(depth):
        lambda_init = 0.8 - 0.6 * math.exp(-0.3 * l)
        proj = _projection(xf, w_in[l].astype(BF16), 1024, 1024)
        oattn = _diff_attention(proj.reshape(bsz, seq, -1), slopes, lambda_q1[l], lambda_k1[l],
                                lambda_q2[l], lambda_k2[l], attn_subln_w[l], lambda_init)
        x1, x1b = _out_projection(oattn.reshape(bsz * seq, -1), proj, xf, conv_w[l], w_out[l],
                                  ln1_g[l], ln1_b[l], seq, alpha)
        kv = _projection(memf, mem_wkv[l].astype(BF16), 1024, 1024)
        x2, x2p, te, tg, rk, counts = _memory_attention(
            x1b, x1, kv, mem_wq[l], mem_wo[l], ln2_g[l], ln2_b[l], router_w[l], router_b[l],
            seq, n_mem, alpha)
        xf = _moe(x2, x2p, te, tg, rk, counts, w_gate[l], b_gate[l], w_up[l], b_up[l],
                  w_down[l], b_down[l], ln3_g[l], ln3_b[l], alpha)
    return xf.reshape(bsz, seq, d)
```

```python
import functools
import math

import jax
import jax.numpy as jnp
from jax import lax
from jax.experimental import pallas as pl
from jax.experimental.pallas import tpu as pltpu

N_ATTN_HEADS = 8
DIFF_HEAD_DIM = 64
HEAD_WIDTH = 2 * DIFF_HEAD_DIM
MEM_HEADS = 4
TOP_K = 4
CONV_TAPS = 3
SWIGLU_LIMIT = 7.0
SWIGLU_ALPHA = 1.702
LN_EPS = 1e-5

LANES = 128
SUBLANES = 8
VMEM_LIMIT = 56 * 1024 * 1024
NEG_BIG = -1e30
ALIBI_SPLIT = 16
ALIBI_OFFSET_LANE = 4
QUERY_BLOCK = 256
ONES_ROWS = 16

F32 = jnp.float32
BF16 = jnp.bfloat16


def _cparams(sem, vmem=VMEM_LIMIT):
    return pltpu.CompilerParams(dimension_semantics=sem, vmem_limit_bytes=vmem)


def _resident(shape, index_map):
    return pl.BlockSpec(shape, index_map, pipeline_mode=pl.Buffered(1))


def _layer_norm(z, g, b):
    mu = jnp.mean(z, axis=-1, keepdims=True)
    d = z - mu
    var = jnp.mean(d * d, axis=-1, keepdims=True)
    return d * lax.rsqrt(var + LN_EPS) * g + b


def _dot(a, b):
    return jnp.dot(a, b, preferred_element_type=F32)


def _dot_nt(a, b):
    return lax.dot_general(a, b, (((1,), (1,)), ((), ())), preferred_element_type=F32)


def _proj_kernel(x_ref, w_ref, o_ref, xb_ref):
    @pl.when(pl.program_id(1) == 0)
    def _():
        xb_ref[...] = x_ref[...].astype(BF16)

    o_ref[...] = _dot(xb_ref[...], w_ref[...]).astype(o_ref.dtype)


def _projection(x, w, tm, tn):
    m, k = x.shape
    n = w.shape[1]
    tm, tn = min(tm, m), min(tn, n)
    return pl.pallas_call(
        _proj_kernel,
        grid=(m // tm, n // tn),
        in_specs=[pl.BlockSpec((tm, k), lambda i, j: (i, 0)),
                  pl.BlockSpec((k, tn), lambda i, j: (0, j))],
        out_specs=pl.BlockSpec((tm, tn), lambda i, j: (i, j)),
        out_shape=jax.ShapeDtypeStruct((m, n), BF16),
        scratch_shapes=[pltpu.VMEM((tm, k), BF16)],
        compiler_params=_cparams(("parallel", "arbitrary")),
    )(x, w)


def _attn_kernel(slopes_ref, q_ref, k_ref, v_ref, qft_ref, kf_ref, bdt_ref, lq1_ref, lk1_ref, lq2_ref,
                 lk2_ref, sub_ref, o_ref, lhst_ref, vt_ref, st_ref, pt_ref, m_ref, alpha_ref, acct_ref,
                 *, tq, tk, seq, lambda_init):
    h = pl.program_id(1)
    qi = pl.program_id(2)
    slope = slopes_ref[h]

    @pl.when(qi == 0)
    def _():
        for c in range(seq // tk):
            vt_ref[c, :HEAD_WIDTH, :] = v_ref[0, c * tk:(c + 1) * tk, :].astype(F32).T.astype(BF16)
            vt_ref[c, HEAD_WIDTH:, :] = jnp.ones((ONES_ROWS, tk), BF16)

    qt = (q_ref[0].astype(F32) * (DIFF_HEAD_DIM ** -0.5)).T
    row = lax.broadcasted_iota(jnp.int32, qt.shape, 0)
    qqt = jnp.concatenate([jnp.where(row < DIFF_HEAD_DIM, qt, 0.0),
                           jnp.where(row >= DIFF_HEAD_DIM, qt, 0.0)], axis=1).astype(BF16)
    fft = jnp.concatenate([qft_ref[0], qft_ref[0]], axis=1)
    lhst_ref[0] = jnp.concatenate([qqt, fft], axis=0)
    lhst_ref[1] = jnp.concatenate([qqt, -fft], axis=0)
    lhst_ref[2] = jnp.concatenate([qqt, jnp.zeros_like(fft)], axis=0)

    m_ref[...] = jnp.full(m_ref.shape, NEG_BIG, F32)
    acct_ref[...] = jnp.zeros(acct_ref.shape, F32)
    kf = kf_ref[0]
    klane = lax.broadcasted_iota(jnp.int32, (tk, HEAD_WIDTH), 1)

    per_q = tq // tk
    n_tiles = seq // tk
    blocks = [slice(qb * QUERY_BLOCK, (qb + 1) * QUERY_BLOCK)
              for qb in range(2 * tq // QUERY_BLOCK)]

    def scores(kt):
        start = pl.multiple_of(kt * tk, tk)
        overlap = kt - qi * per_q
        variant = jnp.where(overlap < 0, 0, jnp.where(overlap >= per_q, 1, 2))
        c = slope * (kt * tk - qi * tq).astype(F32)
        feat = jnp.where(klane == ALIBI_OFFSET_LANE, jnp.full(kf.shape, c, F32).astype(BF16), kf)
        k_aug = jnp.concatenate([k_ref[0, pl.ds(start, tk), :], feat], axis=1)
        for cols in blocks:
            st_ref[:, cols] = _dot(k_aug, lhst_ref[variant, :, cols])

        @pl.when(variant == 2)
        def _():
            bias_t = bdt_ref[0, jnp.clip(overlap, 0, per_q - 1)]
            st_ref[:, :tq] += bias_t
            st_ref[:, tq:] += bias_t

    def probs():
        for cols in blocks:
            st = st_ref[:, cols]
            m_prev = m_ref[:, cols]
            m_new = jnp.maximum(m_prev, jnp.max(st, axis=0, keepdims=True))
            alpha_ref[:, cols] = jnp.exp(m_prev - m_new)
            pt_ref[:, cols] = jnp.exp(st - m_new).astype(BF16)
            m_ref[:, cols] = m_new

    def accumulate(kt):
        vt = vt_ref[kt]
        for cols in blocks:
            acct_ref[:, cols] = alpha_ref[:, cols] * acct_ref[:, cols] + _dot(vt, pt_ref[:, cols])

    def step(j, carry):
        accumulate(j - 2)
        probs()
        scores(j)
        return carry

    scores(0)
    if n_tiles > 1:
        probs()
        scores(1)
        lax.fori_loop(2, n_tiles, step, 0)
        accumulate(n_tiles - 2)
    probs()
    accumulate(n_tiles - 1)

    lam = (jnp.exp(jnp.sum(lq1_ref[...] * lk1_ref[...], keepdims=True))
           - jnp.exp(jnp.sum(lq2_ref[...] * lk2_ref[...], keepdims=True)) + lambda_init)
    acct = acct_ref[...]
    ot = acct[:HEAD_WIDTH, :] / acct[HEAD_WIDTH:HEAD_WIDTH + 1, :]
    ot = ot[:, :tq] - lam * ot[:, tq:]
    ms = jnp.mean(ot * ot, axis=0, keepdims=True)
    yt = ot * lax.rsqrt(ms + LN_EPS) * sub_ref[...] * (1.0 - lambda_init)
    o_ref[0] = yt.T.astype(o_ref.dtype)


def _alibi_features(slopes, t):
    pos = jnp.arange(t, dtype=jnp.int32)
    hi = (pos // ALIBI_SPLIT * ALIBI_SPLIT).astype(F32)[None, :]
    lo = (pos % ALIBI_SPLIT).astype(F32)[None, :]
    sl = slopes[:, None]
    one = jnp.ones((slopes.shape[0], t), F32)
    zero = jnp.zeros_like(one)
    pad = [zero] * (HEAD_WIDTH - 5)
    qf = jnp.stack([one, one, -sl * hi, -sl * lo, one] + pad, axis=-1)
    kf = jnp.stack([sl * hi, sl * lo, one, one, zero] + pad, axis=-1)
    assert ALIBI_OFFSET_LANE == 4
    return qf.astype(BF16), kf.astype(BF16)


def _diff_attention(proj3, slopes, lq1, lk1, lq2, lk2, subln, lambda_init, tq=512, tk=512):
    bsz, seq, _ = proj3.shape
    nh = N_ATTN_HEADS
    tq, tk = min(tq, seq), min(tk, seq)
    assert tq % tk == 0 and seq % tq == 0 and (2 * tq) % QUERY_BLOCK == 0
    qft = _alibi_features(slopes, tq)[0].transpose(0, 2, 1)
    kf = _alibi_features(slopes, tk)[1]
    di = jnp.arange(tq, dtype=jnp.int32)[None, None, :]
    dj = jnp.arange(tk, dtype=jnp.int32)[None, :, None]
    r = jnp.arange(tq // tk, dtype=jnp.int32)[:, None, None]
    bias_t = -slopes[:, None, None, None] * jnp.abs(di - dj - r * tk).astype(F32)[None]
    vec = lambda a: a.reshape(1, -1).astype(F32)
    small = lambda n: pl.BlockSpec((1, n), lambda b, h, i, s: (0, 0))
    grid_spec = pltpu.PrefetchScalarGridSpec(
        num_scalar_prefetch=1,
        grid=(bsz, nh, seq // tq),
        in_specs=[
            pl.BlockSpec((1, tq, HEAD_WIDTH), lambda b, h, i, s: (b, i, h)),
            pl.BlockSpec((1, seq, HEAD_WIDTH), lambda b, h, i, s: (b, 0, nh + h)),
            pl.BlockSpec((1, seq, HEAD_WIDTH), lambda b, h, i, s: (b, 0, 2 * nh + h)),
            pl.BlockSpec((1, HEAD_WIDTH, tq), lambda b, h, i, s: (h, 0, 0)),
            pl.BlockSpec((1, tk, HEAD_WIDTH), lambda b, h, i, s: (h, 0, 0)),
            pl.BlockSpec((1, tq // tk, tk, tq), lambda b, h, i, s: (h, 0, 0, 0)),
            small(DIFF_HEAD_DIM), small(DIFF_HEAD_DIM), small(DIFF_HEAD_DIM), small(DIFF_HEAD_DIM),
            pl.BlockSpec((HEAD_WIDTH, 1), lambda b, h, i, s: (0, 0)),
        ],
        out_specs=pl.BlockSpec((1, tq, HEAD_WIDTH), lambda b, h, i, s: (b, i, h)),
        scratch_shapes=[pltpu.VMEM((3, 2 * HEAD_WIDTH, 2 * tq), BF16),
                        pltpu.VMEM((seq // tk, HEAD_WIDTH + ONES_ROWS, tk), BF16),
                        pltpu.VMEM((tk, 2 * tq), F32),
                        pltpu.VMEM((tk, 2 * tq), BF16),
                        pltpu.VMEM((1, 2 * tq), F32),
                        pltpu.VMEM((1, 2 * tq), F32),
                        pltpu.VMEM((HEAD_WIDTH + ONES_ROWS, 2 * tq), F32)],
    )
    return pl.pallas_call(
        functools.partial(_attn_kernel, tq=tq, tk=tk, seq=seq, lambda_init=lambda_init),
        grid_spec=grid_spec,
        out_shape=jax.ShapeDtypeStruct((bsz, seq, nh * HEAD_WIDTH), BF16),
        compiler_params=_cparams(("parallel", "parallel", "arbitrary")),
    )(slopes, proj3, proj3, proj3, qft, kf, bias_t,
      vec(lq1), vec(lk1), vec(lq2), vec(lk2), subln.reshape(-1, 1).astype(F32))


def _outproj_kernel(oa_ref, gb_ref, gc_ref, u_ref, cp_ref, up_ref, cn_ref, un_ref, cw_ref,
                    wa_ref, wc_ref, x_ref, g_ref, b_ref, x1_ref, x1b_ref, *, tm, seq, alpha):
    i = pl.program_id(0)
    cu = gc_ref[...].astype(F32) * u_ref[...].astype(F32)
    row = lax.broadcasted_iota(jnp.int32, cu.shape, 0)
    last = SUBLANES - 1
    prev_edge = cp_ref[last:last + 1, :].astype(F32) * up_ref[last:last + 1, :].astype(F32)
    next_edge = cn_ref[0:1, :].astype(F32) * un_ref[0:1, :].astype(F32)
    prev_edge = jnp.where((i * tm) % seq == 0, 0.0, prev_edge)
    next_edge = jnp.where(((i + 1) * tm) % seq == 0, 0.0, next_edge)
    cu_prev = jnp.where(row == 0, prev_edge, pltpu.roll(cu, 1, 0))
    cu_next = jnp.where(row == tm - 1, next_edge, pltpu.roll(cu, tm - 1, 0))
    cw = cw_ref[...]
    conv = cu_prev * cw[0:1, :] + cu * cw[1:2, :] + cu_next * cw[2:3, :]
    oc = (gb_ref[...].astype(F32) * conv).astype(BF16)
    y = _dot(oa_ref[...], wa_ref[...]) + _dot(oc, wc_ref[...])
    x1 = _layer_norm(alpha * x_ref[...] + y, g_ref[...], b_ref[...])
    x1_ref[...] = x1
    x1b_ref[...] = x1.astype(BF16)


def _out_projection(oattn, proj, x, conv_w, w_out, ln_g, ln_b, seq, alpha, tm=512):
    t, d = x.shape
    aw = oattn.shape[1]
    cwid = d - aw
    assert aw == cwid and proj.shape[1] == 3 * aw + 3 * cwid
    tm = min(tm, seq)
    hb = tm // SUBLANES
    nhb = t // SUBLANES
    cb = 3 * aw // cwid
    prev_map = lambda c: (lambda i: (jnp.maximum(i * hb - 1, 0), c))
    next_map = lambda c: (lambda i: (jnp.minimum((i + 1) * hb, nhb - 1), c))
    wa = w_out[:aw].astype(BF16)
    wc = w_out[aw:].astype(BF16)
    row = lambda a: a.reshape(1, -1).astype(F32)
    return pl.pallas_call(
        functools.partial(_outproj_kernel, tm=tm, seq=seq, alpha=alpha),
        grid=(t // tm,),
        in_specs=[
            pl.BlockSpec((tm, aw), lambda i: (i, 0)),
            pl.BlockSpec((tm, cwid), lambda i: (i, cb)),
            pl.BlockSpec((tm, cwid), lambda i: (i, cb + 1)),
            pl.BlockSpec((tm, cwid), lambda i: (i, cb + 2)),
            pl.BlockSpec((SUBLANES, cwid), prev_map(cb + 1)),
            pl.BlockSpec((SUBLANES, cwid), prev_map(cb + 2)),
            pl.BlockSpec((SUBLANES, cwid), next_map(cb + 1)),
            pl.BlockSpec((SUBLANES, cwid), next_map(cb + 2)),
            _resident((CONV_TAPS, cwid), lambda i: (0, 0)),
            _resident((aw, d), lambda i: (0, 0)),
            _resident((cwid, d), lambda i: (0, 0)),
            pl.BlockSpec((tm, d), lambda i: (i, 0)),
            _resident((1, d), lambda i: (0, 0)),
            _resident((1, d), lambda i: (0, 0)),
        ],
        out_specs=[pl.BlockSpec((tm, d), lambda i: (i, 0)),
                   pl.BlockSpec((tm, d), lambda i: (i, 0))],
        out_shape=[jax.ShapeDtypeStruct((t, d), F32), jax.ShapeDtypeStruct((t, d), BF16)],
        compiler_params=_cparams(("parallel",)),
    )(oattn, proj, proj, proj, proj, proj, proj, proj, conv_w.astype(F32), wa, wc, x,
      row(ln_g), row(ln_b))


def _memattn_kernel(x1b_ref, x1_ref, wq_ref, kv_ref, wo_ref, g_ref, b_ref, rwh_ref, rwl_ref, rb_ref,
                    x2_ref, x2p_ref, te_ref, tg_ref, rk_ref, cnt_ref, carry_ref, *, tm, alpha):
    i = pl.program_id(0)
    d = x1_ref.shape[1]
    hd = d // MEM_HEADS

    @pl.when(i == 0)
    def _():
        carry_ref[...] = jnp.zeros(carry_ref.shape, F32)

    q = _dot(x1b_ref[...], wq_ref[...]).astype(BF16)
    scale = hd ** -0.5
    heads = []
    for hh in range(MEM_HEADS):
        qh = q[:, hh * hd:(hh + 1) * hd]
        kh = kv_ref[:, hh * hd:(hh + 1) * hd]
        vh = kv_ref[:, d + hh * hd:d + (hh + 1) * hd]
        s = _dot_nt(qh, kh) * scale
        p = jnp.exp(s - jnp.max(s, axis=1, keepdims=True))
        p = p / jnp.sum(p, axis=1, keepdims=True)
        heads.append(_dot(p.astype(BF16), vh).astype(BF16))
    xa = _dot(jnp.concatenate(heads, axis=1), wo_ref[...])
    x2 = _layer_norm(alpha * x1_ref[...] + xa, g_ref[...], b_ref[...])
    x2_ref[...] = x2

    half = d // 2
    lo = lax.bitcast_convert_type(x2[:, :half].astype(BF16).astype(F32), jnp.uint32)
    hi = lax.bitcast_convert_type(x2[:, half:].astype(BF16).astype(F32), jnp.uint32)
    x2p_ref[...] = (lo >> 16) | (hi & jnp.uint32(0xFFFF0000))

    xh = x2.astype(BF16)
    xl = (x2 - xh.astype(F32)).astype(BF16)
    logits = (_dot(xh, rwh_ref[...]) + _dot(xh, rwl_ref[...]) + _dot(xl, rwh_ref[...])
              + rb_ref[...])

    lane = lax.broadcasted_iota(jnp.int32, logits.shape, 1)
    work = logits
    vals, sels, idxs = [], [], []
    for _ in range(TOP_K):
        mk = jnp.max(work, axis=1, keepdims=True)
        idx = jnp.min(jnp.where(work == mk, lane, LANES), axis=1, keepdims=True)
        sel = lane == idx
        vals.append(mk)
        idxs.append(idx)
        sels.append(sel)
        work = jnp.where(sel, -jnp.inf, work)
    exps = [jnp.exp(v - vals[0]) for v in vals]
    denom = exps[0]
    for e in exps[1:]:
        denom = denom + e

    onehot = jnp.zeros(logits.shape, F32)
    for sel in sels:
        onehot = onehot + sel.astype(F32)
    lower = (lax.broadcasted_iota(jnp.int32, (tm, tm), 0)
             > lax.broadcasted_iota(jnp.int32, (tm, tm), 1)).astype(BF16)
    before = _dot(lower, onehot.astype(BF16)) + carry_ref[...]

    te = jnp.zeros(logits.shape, jnp.int32)
    tg = jnp.zeros(logits.shape, F32)
    rk = jnp.zeros(logits.shape, jnp.int32)
    for k in range(TOP_K):
        rank_k = jnp.sum(jnp.where(sels[k], before, 0.0), axis=1, keepdims=True)
        te = jnp.where(lane == k, idxs[k], te)
        tg = jnp.where(lane == k, exps[k] / denom, tg)
        rk = jnp.where(lane == k, rank_k.astype(jnp.int32), rk)
    te_ref[...] = te
    tg_ref[...] = tg
    rk_ref[...] = rk
    carry_ref[...] = carry_ref[...] + jnp.sum(onehot, axis=0, keepdims=True)
    cnt_ref[...] = carry_ref[...]


def _memory_attention(x1b, x1, kv, wq, wo, ln_g, ln_b, router_w, router_b, seq, n_mem, alpha, tm=256):
    t, d = x1.shape
    tm = min(tm, seq)
    ne = router_w.shape[1]
    assert ne <= LANES
    rw = jnp.zeros((d, LANES), F32).at[:, :ne].set(router_w.astype(F32))
    rwh = rw.astype(BF16)
    rwl = (rw - rwh.astype(F32)).astype(BF16)
    rb = jnp.full((1, LANES), NEG_BIG, F32).at[0, :ne].set(router_b.astype(F32))
    row = lambda a: a.reshape(1, -1).astype(F32)
    tiles_per_seq = seq // tm
    tile_out = lambda w, dt: (pl.BlockSpec((tm, w), lambda i: (i, 0)), jax.ShapeDtypeStruct((t, w), dt))
    outs = [tile_out(d, F32), tile_out(d // 2, jnp.uint32), tile_out(LANES, jnp.int32),
            tile_out(LANES, F32), tile_out(LANES, jnp.int32),
            (pl.BlockSpec((1, LANES), lambda i: (0, 0)), jax.ShapeDtypeStruct((1, LANES), F32))]
    return pl.pallas_call(
        functools.partial(_memattn_kernel, tm=tm, alpha=alpha),
        grid=(t // tm,),
        in_specs=[
            pl.BlockSpec((tm, d), lambda i: (i, 0)),
            pl.BlockSpec((tm, d), lambda i: (i, 0)),
            _resident((d, d), lambda i: (0, 0)),
            pl.BlockSpec((n_mem, 2 * d), lambda i: (i // tiles_per_seq, 0)),
            _resident((d, d), lambda i: (0, 0)),
            _resident((1, d), lambda i: (0, 0)),
            _resident((1, d), lambda i: (0, 0)),
            _resident((d, LANES), lambda i: (0, 0)),
            _resident((d, LANES), lambda i: (0, 0)),
            _resident((1, LANES), lambda i: (0, 0)),
        ],
        out_specs=[o[0] for o in outs],
        out_shape=[o[1] for o in outs],
        scratch_shapes=[pltpu.VMEM((1, LANES), F32)],
        compiler_params=_cparams(("arbitrary",)),
    )(x1b, x1, wq.astype(BF16), kv, wo.astype(BF16), row(ln_g), row(ln_b), rwh, rwl, rb)


def _dispatch_kernel(dest_ref, x2p_ref, init_ref, xs_ref, sem, *, rows):
    del init_ref
    base = pl.program_id(0) * rows * TOP_K

    def issue(r, carry):
        for k in range(TOP_K):
            dst = dest_ref[base + r * TOP_K + k]
            pltpu.make_async_copy(x2p_ref.at[pl.ds(r, 1)], xs_ref.at[pl.ds(dst, 1)], sem).start()
        return carry

    lax.fori_loop(0, rows, issue, 0)
    for _ in range(TOP_K):
        pltpu.make_async_copy(x2p_ref, xs_ref.at[pl.ds(0, rows)], sem).wait()


def _dispatch(dest, x2p, n_rows, rows=512):
    t, w = x2p.shape
    rows = min(rows, t)
    init = jnp.zeros((n_rows, w), x2p.dtype)
    grid_spec = pltpu.PrefetchScalarGridSpec(
        num_scalar_prefetch=1,
        grid=(t // rows,),
        in_specs=[pl.BlockSpec((rows, w), lambda i, s: (i, 0)),
                  pl.BlockSpec(memory_space=pl.ANY)],
        out_specs=pl.BlockSpec(memory_space=pl.ANY),
        scratch_shapes=[pltpu.SemaphoreType.DMA],
    )
    return pl.pallas_call(
        functools.partial(_dispatch_kernel, rows=rows),
        grid_spec=grid_spec,
        out_shape=jax.ShapeDtypeStruct((n_rows, w), x2p.dtype),
        input_output_aliases={2: 0},
        compiler_params=_cparams(("arbitrary",)),
    )(dest, x2p, init)


def _unpack_rows(words):
    lo = lax.bitcast_convert_type(words << 16, F32).astype(BF16)
    hi = lax.bitcast_convert_type(words & jnp.uint32(0xFFFF0000), F32).astype(BF16)
    return lo, hi


def _expert_up_kernel(ie_ref, in_ref, irt_ref, ivalid_ref, ifirst_ref, xs_ref, wg_ref, wu_ref,
                      bg_ref, bu_ref, h_ref, wgb_ref, wub_ref):
    w = pl.program_id(0)

    @pl.when(ifirst_ref[w] == 1)
    def _():
        wgb_ref[...] = wg_ref[0].astype(BF16)
        wub_ref[...] = wu_ref[0].astype(BF16)

    @pl.when(ivalid_ref[w] == 1)
    def _():
        lo, hi = _unpack_rows(xs_ref[...])
        half = lo.shape[1]
        gate = _dot(lo, wgb_ref[:half, :]) + _dot(hi, wgb_ref[half:, :]) + bg_ref[0]
        up = _dot(lo, wub_ref[:half, :]) + _dot(hi, wub_ref[half:, :]) + bu_ref[0]
        gate = jnp.minimum(gate, SWIGLU_LIMIT)
        up = jnp.clip(up, -SWIGLU_LIMIT, SWIGLU_LIMIT)
        act = gate * jax.nn.sigmoid(SWIGLU_ALPHA * gate) * (up + 1.0)
        h_ref[...] = act.astype(h_ref.dtype)

    @pl.when(ivalid_ref[w] == 0)
    def _():
        h_ref[...] = jnp.zeros(h_ref.shape, h_ref.dtype)


def _expert_down_kernel(ie_ref, in_ref, irt_ref, ivalid_ref, ifirst_ref, h_ref, wd_ref, bd_ref,
                        o_ref, wdb_ref):
    w = pl.program_id(0)

    @pl.when(ifirst_ref[w] == 1)
    def _():
        wdb_ref[...] = wd_ref[0].astype(BF16)

    @pl.when(ivalid_ref[w] == 1)
    def _():
        o_ref[...] = _dot(h_ref[...], wdb_ref[...]) + bd_ref[0]

    @pl.when(ivalid_ref[w] == 0)
    def _():
        o_ref[...] = jnp.zeros(o_ref.shape, o_ref.dtype)


def _work_items(tiles_e, n_tiles_max, n_col):
    ne = tiles_e.shape[0]
    cum = jnp.cumsum(tiles_e)
    total_tiles = cum[-1]
    first_tile = cum - tiles_e
    item_end = cum * n_col
    w = jnp.arange(n_tiles_max * n_col, dtype=jnp.int32)
    e = jnp.minimum(jnp.searchsorted(item_end, w, side='right'), ne - 1).astype(jnp.int32)
    te = jnp.maximum(tiles_e[e], 1)
    local = w - (item_end[e] - tiles_e[e] * n_col)
    valid = w < total_tiles * n_col
    spare = jnp.maximum(n_tiles_max - total_tiles, 1)
    j = w - total_tiles * n_col
    col = jnp.where(valid, local // te, j // spare)
    rt = jnp.where(valid, first_tile[e] + local % te, total_tiles + j % spare)
    first = jnp.where(valid, (local % te) == 0, False)
    i32 = lambda a: a.astype(jnp.int32)
    return i32(e), i32(col), i32(rt), i32(valid), i32(first)


def _expert_up(items, xs, w_gate, w_up, b_gate, b_up, tm, tn):
    n_rows, half = xs.shape
    ne, d, f = w_gate.shape
    tn = min(tn, f)
    n_items = items[0].shape[0]
    wspec = pl.BlockSpec((1, d, tn), lambda w, ie, ic, irt, iv, ifi: (ie[w], 0, ic[w]))
    bspec = pl.BlockSpec((1, 1, tn), lambda w, ie, ic, irt, iv, ifi: (ie[w], 0, ic[w]))
    grid_spec = pltpu.PrefetchScalarGridSpec(
        num_scalar_prefetch=5,
        grid=(n_items,),
        in_specs=[pl.BlockSpec((tm, half), lambda w, ie, ic, irt, iv, ifi: (irt[w], 0)),
                  wspec, wspec, bspec, bspec],
        out_specs=pl.BlockSpec((tm, tn), lambda w, ie, ic, irt, iv, ifi: (irt[w], ic[w])),
        scratch_shapes=[pltpu.VMEM((d, tn), BF16), pltpu.VMEM((d, tn), BF16)],
    )
    return pl.pallas_call(
        _expert_up_kernel,
        grid_spec=grid_spec,
        out_shape=jax.ShapeDtypeStruct((n_rows, f), BF16),
        compiler_params=_cparams(("arbitrary",)),
    )(*items, xs, w_gate, w_up, b_gate.reshape(ne, 1, f), b_up.reshape(ne, 1, f))


def _expert_down(items, h, w_down, b_down, tm, tn):
    n_rows, f = h.shape
    ne, _, d = w_down.shape
    tn = min(tn, d)
    n_items = items[0].shape[0]
    grid_spec = pltpu.PrefetchScalarGridSpec(
        num_scalar_prefetch=5,
        grid=(n_items,),
        in_specs=[pl.BlockSpec((tm, f), lambda w, ie, ic, irt, iv, ifi: (irt[w], 0)),
                  pl.BlockSpec((1, f, tn), lambda w, ie, ic, irt, iv, ifi: (ie[w], 0, ic[w])),
                  pl.BlockSpec((1, 1, tn), lambda w, ie, ic, irt, iv, ifi: (ie[w], 0, ic[w]))],
        out_specs=pl.BlockSpec((tm, tn), lambda w, ie, ic, irt, iv, ifi: (irt[w], ic[w])),
        scratch_shapes=[pltpu.VMEM((f, tn), BF16)],
    )
    return pl.pallas_call(
        _expert_down_kernel,
        grid_spec=grid_spec,
        out_shape=jax.ShapeDtypeStruct((n_rows, d), F32),
        compiler_params=_cparams(("arbitrary",)),
    )(*items, h, w_down, b_down.reshape(ne, 1, d))


def _combine_kernel(dest_ref, os_ref, tg_ref, x2_ref, g_ref, b_ref, o_ref, buf_ref, sem, *, rows, alpha):
    base = pl.program_id(0) * rows * TOP_K

    def issue(r, carry):
        for k in range(TOP_K):
            src = dest_ref[base + r * TOP_K + k]
            pltpu.make_async_copy(os_ref.at[pl.ds(src, 1)], buf_ref.at[k, pl.ds(r, 1)], sem).start()
        return carry

    lax.fori_loop(0, rows, issue, 0)
    for k in range(TOP_K):
        pltpu.make_async_copy(os_ref.at[pl.ds(0, rows)], buf_ref.at[k], sem).wait()

    tg = tg_ref[...]
    y = tg[:, 0:1] * buf_ref[0]
    for k in range(1, TOP_K):
        y = y + tg[:, k:k + 1] * buf_ref[k]
    o_ref[...] = _layer_norm(alpha * x2_ref[...] + y, g_ref[...], b_ref[...])


def _combine(dest, out_sorted, tg, x2, ln_g, ln_b, alpha, rows=256):
    t, d = x2.shape
    rows = min(rows, t)
    row = lambda a: a.reshape(1, -1).astype(F32)
    grid_spec = pltpu.PrefetchScalarGridSpec(
        num_scalar_prefetch=1,
        grid=(t // rows,),
        in_specs=[pl.BlockSpec(memory_space=pl.ANY),
                  pl.BlockSpec((rows, LANES), lambda i, s: (i, 0)),
                  pl.BlockSpec((rows, d), lambda i, s: (i, 0)),
                  pl.BlockSpec((1, d), lambda i, s: (0, 0)),
                  pl.BlockSpec((1, d), lambda i, s: (0, 0))],
        out_specs=pl.BlockSpec((rows, d), lambda i, s: (i, 0)),
        scratch_shapes=[pltpu.VMEM((TOP_K, rows, d), F32), pltpu.SemaphoreType.DMA],
    )
    return pl.pallas_call(
        functools.partial(_combine_kernel, rows=rows, alpha=alpha),
        grid_spec=grid_spec,
        out_shape=jax.ShapeDtypeStruct((t, d), F32),
        compiler_params=_cparams(("arbitrary",)),
    )(dest, out_sorted, tg, x2, row(ln_g), row(ln_b))


def _moe(x2, x2p, te, tg, rk, counts, w_gate, b_gate, w_up, b_up, w_down, b_down, ln_g, ln_b,
         alpha, tm=512, tn_up=512, tn_down=1024):
    t, d = x2.shape
    ne = w_gate.shape[0]
    n_assign = t * TOP_K
    tm = min(tm, t)
    n_tiles_max = -(-n_assign // tm) + ne
    n_rows = n_tiles_max * tm

    counts = counts[0, :ne].astype(jnp.int32)
    tiles_e = (counts + tm - 1) // tm
    pad_start = (jnp.cumsum(tiles_e) - tiles_e) * tm
    dest = (pad_start[te[:, :TOP_K]] + rk[:, :TOP_K]).reshape(n_assign).astype(jnp.int32)

    xs = _dispatch(dest, x2p, n_rows)
    f = w_gate.shape[2]
    items_up = _work_items(tiles_e, n_tiles_max, f // min(tn_up, f))
    h = _expert_up(items_up, xs, w_gate, w_up, b_gate, b_up, tm, tn_up)
    items_dn = _work_items(tiles_e, n_tiles_max, d // min(tn_down, d))
    out_sorted = _expert_down(items_dn, h, w_down, b_down, tm, tn_down)
    return _combine(dest, out_sorted, tg, x2, ln_g, ln_b, alpha)


def kernel(x, mem, w_in, conv_w, attn_subln_w, lambda_q1, lambda_k1, lambda_q2, lambda_k2, w_out, ln1_g, ln1_b, mem_wq, mem_wkv, mem_wo, ln2_g, ln2_b, router_w, router_b, w_gate, b_gate, w_up, b_up, w_down, b_down, ln3_g, ln3_b):
    bsz, seq, d = x.shape
    n_mem = mem.shape[1]
    depth = w_in.shape[0]
    alpha = (2.0 * depth) ** 0.25
    slopes = jnp.exp2(-8.0 / N_ATTN_HEADS * jnp.arange(1, N_ATTN_HEADS + 1, dtype=F32))
    xf = x.reshape(bsz * seq, d)
    memf = mem.reshape(bsz * n_mem, d)
    for l in range(depth):
        lambda_init = 0.8 - 0.6 * math.exp(-0.3 * l)
        proj = _projection(xf, w_in[l].astype(BF16), 1024, 1024)
        oattn = _diff_attention(proj.reshape(bsz, seq, -1), slopes, lambda_q1[l], lambda_k1[l],
                                lambda_q2[l], lambda_k2[l], attn_subln_w[l], lambda_init)
        x1, x1b = _out_projection(oattn.reshape(bsz * seq, -1), proj, xf, conv_w[l], w_out[l],
                                  ln1_g[l], ln1_b[l], seq, alpha)
        kv = _projection(memf, mem_wkv[l].astype(BF16), 1024, 1024)
        x2, x2p, te, tg, rk, counts = _memory_attention(
            x1b, x1, kv, mem_wq[l], mem_wo[l], ln2_g[l], ln2_b[l], router_w[l], router_b[l],
            seq, n_mem, alpha)
        xf = _moe(x2, x2p, te, tg, rk, counts, w_gate[l], b_gate[l], w_up[l], b_up[l],
                  w_down[l], b_down[l], ln3_g[l], ln3_b[l], alpha)
    return xf.reshape(bsz, seq, d)
```

```python
import functools
import math

import jax
import jax.numpy as jnp
from jax import lax
from jax.experimental import pallas as pl
from jax.experimental.pallas import tpu as pltpu

N_ATTN_HEADS = 8
DIFF_HEAD_DIM = 64
HEAD_WIDTH = 2 * DIFF_HEAD_DIM
MEM_HEADS = 4
TOP_K = 4
CONV_TAPS = 3
SWIGLU_LIMIT = 7.0
SWIGLU_ALPHA = 1.702
LN_EPS = 1e-5

LANES = 128
SUBLANES = 8
VMEM_LIMIT = 56 * 1024 * 1024
NEG_BIG = -1e30
ALIBI_SPLIT = 16
ALIBI_OFFSET_LANE = 4
QUERY_BLOCK = 256
ONES_ROWS = 16
UNDERFLOW_NATS = 104.0
NORM_SLACK = 1.01

F32 = jnp.float32
BF16 = jnp.bfloat16


def _cparams(sem, vmem=VMEM_LIMIT):
    return pltpu.CompilerParams(dimension_semantics=sem, vmem_limit_bytes=vmem)


def _resident(shape, index_map):
    return pl.BlockSpec(shape, index_map, pipeline_mode=pl.Buffered(1))


def _layer_norm(z, g, b):
    mu = jnp.mean(z, axis=-1, keepdims=True)
    d = z - mu
    var = jnp.mean(d * d, axis=-1, keepdims=True)
    return d * lax.rsqrt(var + LN_EPS) * g + b


def _dot(a, b):
    return jnp.dot(a, b, preferred_element_type=F32)


def _dot_nt(a, b):
    return lax.dot_general(a, b, (((1,), (1,)), ((), ())), preferred_element_type=F32)


def _proj_kernel(x_ref, w_ref, o_ref, xb_ref):
    @pl.when(pl.program_id(1) == 0)
    def _():
        xb_ref[...] = x_ref[...].astype(BF16)

    o_ref[...] = _dot(xb_ref[...], w_ref[...]).astype(o_ref.dtype)


def _projection(x, w, tm, tn):
    m, k = x.shape
    n = w.shape[1]
    tm, tn = min(tm, m), min(tn, n)
    return pl.pallas_call(
        _proj_kernel,
        grid=(m // tm, n // tn),
        in_specs=[pl.BlockSpec((tm, k), lambda i, j: (i, 0)),
                  pl.BlockSpec((k, tn), lambda i, j: (0, j))],
        out_specs=pl.BlockSpec((tm, tn), lambda i, j: (i, j)),
        out_shape=jax.ShapeDtypeStruct((m, n), BF16),
        scratch_shapes=[pltpu.VMEM((tm, k), BF16)],
        compiler_params=_cparams(("parallel", "arbitrary")),
    )(x, w)


def _attn_kernel(slopes_ref, q_ref, k_ref, v_ref, qft_ref, kf_ref, bdt_ref, lq1_ref, lk1_ref, lq2_ref,
                 lk2_ref, sub_ref, o_ref, lhst_ref, vt_ref, st_ref, pt_ref, m_ref, mx_ref, alpha_ref,
                 acct_ref, kn_ref, *, tq, tk, seq, lambda_init):
    h = pl.program_id(1)
    qi = pl.program_id(2)
    slope = slopes_ref[h]

    @pl.when(qi == 0)
    def _():
        half_lane = lax.broadcasted_iota(jnp.int32, (tk, HEAD_WIDTH), 1) < DIFF_HEAD_DIM
        kn1 = jnp.zeros((1, 1), F32)
        kn2 = jnp.zeros((1, 1), F32)
        for c in range(seq // tk):
            vt_ref[c, :HEAD_WIDTH, :] = v_ref[0, c * tk:(c + 1) * tk, :].astype(F32).T.astype(BF16)
            vt_ref[c, HEAD_WIDTH:, :] = jnp.ones((ONES_ROWS, tk), BF16)
            kc = k_ref[0, c * tk:(c + 1) * tk, :].astype(F32)
            sq = kc * kc
            n1 = jnp.sum(jnp.where(half_lane, sq, 0.0), axis=1, keepdims=True)
            n2 = jnp.sum(jnp.where(half_lane, 0.0, sq), axis=1, keepdims=True)
            kn1 = jnp.maximum(kn1, jnp.max(n1, axis=0, keepdims=True))
            kn2 = jnp.maximum(kn2, jnp.max(n2, axis=0, keepdims=True))
        kn_ref[...] = jnp.where(lax.broadcasted_iota(jnp.int32, kn_ref.shape, 1) == 0, kn1, kn2)

    qt = (q_ref[0].astype(F32) * (DIFF_HEAD_DIM ** -0.5)).T
    row = lax.broadcasted_iota(jnp.int32, qt.shape, 0)

    qsq = qt * qt
    qn1 = jnp.max(jnp.sum(jnp.where(row < DIFF_HEAD_DIM, qsq, 0.0), axis=0, keepdims=True),
                  axis=1, keepdims=True)
    qn2 = jnp.max(jnp.sum(jnp.where(row >= DIFF_HEAD_DIM, qsq, 0.0), axis=0, keepdims=True),
                  axis=1, keepdims=True)
    kn = kn_ref[...]
    qk = jnp.sqrt(jnp.maximum(qn1 * kn[:, 0:1], qn2 * kn[:, 1:2]))
    reach = jnp.minimum((UNDERFLOW_NATS + 2.0 * NORM_SLACK * qk) / slope, 4.0 * seq)
    q_lo = (qi * tq).astype(F32)
    lo_f = jnp.floor((q_lo - (tk - 1) - reach) / tk) + 1.0
    hi_f = jnp.ceil((q_lo + (tq - 1) + reach) / tk)
    n_tiles = seq // tk
    per_q = tq // tk
    lo = jnp.clip(lo_f.astype(jnp.int32)[0, 0], 0, qi * per_q)
    hi = jnp.clip(hi_f.astype(jnp.int32)[0, 0], (qi + 1) * per_q, n_tiles)
    if n_tiles > 1:
        short = hi - lo < 2
        lo, hi = (jnp.where(short & (lo > 0), lo - 1, lo), jnp.where(short & (lo == 0), hi + 1, hi))
    qqt = jnp.concatenate([jnp.where(row < DIFF_HEAD_DIM, qt, 0.0),
                           jnp.where(row >= DIFF_HEAD_DIM, qt, 0.0)], axis=1).astype(BF16)
    fft = jnp.concatenate([qft_ref[0], qft_ref[0]], axis=1)
    lhst_ref[0] = jnp.concatenate([qqt, fft], axis=0)
    lhst_ref[1] = jnp.concatenate([qqt, -fft], axis=0)
    lhst_ref[2] = jnp.concatenate([qqt, jnp.zeros_like(fft)], axis=0)

    m_ref[...] = jnp.full(m_ref.shape, NEG_BIG, F32)
    acct_ref[...] = jnp.zeros(acct_ref.shape, F32)
    kf = kf_ref[0]
    klane = lax.broadcasted_iota(jnp.int32, (tk, HEAD_WIDTH), 1)

    blocks = [slice(qb * QUERY_BLOCK, (qb + 1) * QUERY_BLOCK)
              for qb in range(2 * tq // QUERY_BLOCK)]

    def scores(kt):
        start = pl.multiple_of(kt * tk, tk)
        overlap = kt - qi * per_q
        variant = jnp.where(overlap < 0, 0, jnp.where(overlap >= per_q, 1, 2))
        c = slope * (kt * tk - qi * tq).astype(F32)
        feat = jnp.where(klane == ALIBI_OFFSET_LANE, jnp.full(kf.shape, c, F32).astype(BF16), kf)
        k_aug = jnp.concatenate([k_ref[0, pl.ds(start, tk), :], feat], axis=1)
        for cols in blocks:
            st = _dot(k_aug, lhst_ref[variant, :, cols])
            st_ref[:, cols] = st
            mx_ref[:, cols] = jnp.max(st, axis=0, keepdims=True)

        @pl.when(variant == 2)
        def _():
            bias_t = bdt_ref[0, jnp.clip(overlap, 0, per_q - 1)]
            for cols in blocks:
                di = cols.start % tq
                st = st_ref[:, cols] + bias_t[:, di:di + QUERY_BLOCK]
                st_ref[:, cols] = st
                mx_ref[:, cols] = jnp.max(st, axis=0, keepdims=True)

    def probs():
        for cols in blocks:
            st = st_ref[:, cols]
            m_prev = m_ref[:, cols]
            m_new = jnp.maximum(m_prev, mx_ref[:, cols])
            alpha_ref[:, cols] = jnp.exp(m_prev - m_new)
            pt_ref[:, cols] = jnp.exp(st - m_new).astype(BF16)
            m_ref[:, cols] = m_new

    def accumulate(kt):
        vt = vt_ref[kt]
        for cols in blocks:
            acct_ref[:, cols] = alpha_ref[:, cols] * acct_ref[:, cols] + _dot(vt, pt_ref[:, cols])

    def step(j, carry):
        accumulate(j - 2)
        probs()
        scores(j)
        return carry

    scores(lo)
    if n_tiles > 1:
        probs()
        scores(lo + 1)
        lax.fori_loop(lo + 2, hi, step, 0)
        accumulate(hi - 2)
    probs()
    accumulate(hi - 1)

    lam = (jnp.exp(jnp.sum(lq1_ref[...] * lk1_ref[...], keepdims=True))
           - jnp.exp(jnp.sum(lq2_ref[...] * lk2_ref[...], keepdims=True)) + lambda_init)
    acct = acct_ref[...]
    ot = acct[:HEAD_WIDTH, :] / acct[HEAD_WIDTH:HEAD_WIDTH + 1, :]
    ot = ot[:, :tq] - lam * ot[:, tq:]
    ms = jnp.mean(ot * ot, axis=0, keepdims=True)
    yt = ot * lax.rsqrt(ms + LN_EPS) * sub_ref[...] * (1.0 - lambda_init)
    o_ref[0] = yt.T.astype(o_ref.dtype)


def _alibi_features(slopes, t):
    pos = jnp.arange(t, dtype=jnp.int32)
    hi = (pos // ALIBI_SPLIT * ALIBI_SPLIT).astype(F32)[None, :]
    lo = (pos % ALIBI_SPLIT).astype(F32)[None, :]
    sl = slopes[:, None]
    one = jnp.ones((slopes.shape[0], t), F32)
    zero = jnp.zeros_like(one)
    pad = [zero] * (HEAD_WIDTH - 5)
    qf = jnp.stack([one, one, -sl * hi, -sl * lo, one] + pad, axis=-1)
    kf = jnp.stack([sl * hi, sl * lo, one, one, zero] + pad, axis=-1)
    assert ALIBI_OFFSET_LANE == 4
    return qf.astype(BF16), kf.astype(BF16)


def _diff_attention(proj3, slopes, lq1, lk1, lq2, lk2, subln, lambda_init, tq=512, tk=512):
    bsz, seq, _ = proj3.shape
    nh = N_ATTN_HEADS
    tq, tk = min(tq, seq), min(tk, seq)
    assert tq % tk == 0 and seq % tq == 0 and (2 * tq) % QUERY_BLOCK == 0
    qft = _alibi_features(slopes, tq)[0].transpose(0, 2, 1)
    kf = _alibi_features(slopes, tk)[1]
    di = jnp.arange(tq, dtype=jnp.int32)[None, None, :]
    dj = jnp.arange(tk, dtype=jnp.int32)[None, :, None]
    r = jnp.arange(tq // tk, dtype=jnp.int32)[:, None, None]
    bias_t = -slopes[:, None, None, None] * jnp.abs(di - dj - r * tk).astype(F32)[None]
    vec = lambda a: a.reshape(1, -1).astype(F32)
    small = lambda n: pl.BlockSpec((1, n), lambda b, h, i, s: (0, 0))
    grid_spec = pltpu.PrefetchScalarGridSpec(
        num_scalar_prefetch=1,
        grid=(bsz, nh, seq // tq),
        in_specs=[
            pl.BlockSpec((1, tq, HEAD_WIDTH), lambda b, h, i, s: (b, i, h)),
            pl.BlockSpec((1, seq, HEAD_WIDTH), lambda b, h, i, s: (b, 0, nh + h)),
            pl.BlockSpec((1, seq, HEAD_WIDTH), lambda b, h, i, s: (b, 0, 2 * nh + h)),
            pl.BlockSpec((1, HEAD_WIDTH, tq), lambda b, h, i, s: (h, 0, 0)),
            pl.BlockSpec((1, tk, HEAD_WIDTH), lambda b, h, i, s: (h, 0, 0)),
            pl.BlockSpec((1, tq // tk, tk, tq), lambda b, h, i, s: (h, 0, 0, 0)),
            small(DIFF_HEAD_DIM), small(DIFF_HEAD_DIM), small(DIFF_HEAD_DIM), small(DIFF_HEAD_DIM),
            pl.BlockSpec((HEAD_WIDTH, 1), lambda b, h, i, s: (0, 0)),
        ],
        out_specs=pl.BlockSpec((1, tq, HEAD_WIDTH), lambda b, h, i, s: (b, i, h)),
        scratch_shapes=[pltpu.VMEM((3, 2 * HEAD_WIDTH, 2 * tq), BF16),
                        pltpu.VMEM((seq // tk, HEAD_WIDTH + ONES_ROWS, tk), BF16),
                        pltpu.VMEM((tk, 2 * tq), F32),
                        pltpu.VMEM((tk, 2 * tq), BF16),
                        pltpu.VMEM((1, 2 * tq), F32),
                        pltpu.VMEM((1, 2 * tq), F32),
                        pltpu.VMEM((1, 2 * tq), F32),
                        pltpu.VMEM((HEAD_WIDTH + ONES_ROWS, 2 * tq), F32),
                        pltpu.VMEM((1, LANES), F32)],
    )
    return pl.pallas_call(
        functools.partial(_attn_kernel, tq=tq, tk=tk, seq=seq, lambda_init=lambda_init),
        grid_spec=grid_spec,
        out_shape=jax.ShapeDtypeStruct((bsz, seq, nh * HEAD_WIDTH), BF16),
        compiler_params=_cparams(("parallel", "parallel", "arbitrary")),
    )(slopes, proj3, proj3, proj3, qft, kf, bias_t,
      vec(lq1), vec(lk1), vec(lq2), vec(lk2), subln.reshape(-1, 1).astype(F32))


def _outproj_kernel(oa_ref, gb_ref, gc_ref, u_ref, cp_ref, up_ref, cn_ref, un_ref, cw_ref,
                    wa_ref, wc_ref, x_ref, g_ref, b_ref, x1_ref, x1b_ref, *, tm, seq, alpha):
    i = pl.program_id(0)
    cu = gc_ref[...].astype(F32) * u_ref[...].astype(F32)
    row = lax.broadcasted_iota(jnp.int32, cu.shape, 0)
    last = SUBLANES - 1
    prev_edge = cp_ref[last:last + 1, :].astype(F32) * up_ref[last:last + 1, :].astype(F32)
    next_edge = cn_ref[0:1, :].astype(F32) * un_ref[0:1, :].astype(F32)
    prev_edge = jnp.where((i * tm) % seq == 0, 0.0, prev_edge)
    next_edge = jnp.where(((i + 1) * tm) % seq == 0, 0.0, next_edge)
    cu_prev = jnp.where(row == 0, prev_edge, pltpu.roll(cu, 1, 0))
    cu_next = jnp.where(row == tm - 1, next_edge, pltpu.roll(cu, tm - 1, 0))
    cw = cw_ref[...]
    conv = cu_prev * cw[0:1, :] + cu * cw[1:2, :] + cu_next * cw[2:3, :]
    oc = (gb_ref[...].astype(F32) * conv).astype(BF16)
    y = _dot(oa_ref[...], wa_ref[...]) + _dot(oc, wc_ref[...])
    x1 = _layer_norm(alpha * x_ref[...] + y, g_ref[...], b_ref[...])
    x1_ref[...] = x1
    x1b_ref[...] = x1.astype(BF16)


def _out_projection(oattn, proj, x, conv_w, w_out, ln_g, ln_b, seq, alpha, tm=512):
    t, d = x.shape
    aw = oattn.shape[1]
    cwid = d - aw
    assert aw == cwid and proj.shape[1] == 3 * aw + 3 * cwid
    tm = min(tm, seq)
    hb = tm // SUBLANES
    nhb = t // SUBLANES
    cb = 3 * aw // cwid
    prev_map = lambda c: (lambda i: (jnp.maximum(i * hb - 1, 0), c))
    next_map = lambda c: (lambda i: (jnp.minimum((i + 1) * hb, nhb - 1), c))
    wa = w_out[:aw].astype(BF16)
    wc = w_out[aw:].astype(BF16)
    row = lambda a: a.reshape(1, -1).astype(F32)
    return pl.pallas_call(
        functools.partial(_outproj_kernel, tm=tm, seq=seq, alpha=alpha),
        grid=(t // tm,),
        in_specs=[
            pl.BlockSpec((tm, aw), lambda i: (i, 0)),
            pl.BlockSpec((tm, cwid), lambda i: (i, cb)),
            pl.BlockSpec((tm, cwid), lambda i: (i, cb + 1)),
            pl.BlockSpec((tm, cwid), lambda i: (i, cb + 2)),
            pl.BlockSpec((SUBLANES, cwid), prev_map(cb + 1)),
            pl.BlockSpec((SUBLANES, cwid), prev_map(cb + 2)),
            pl.BlockSpec((SUBLANES, cwid), next_map(cb + 1)),
            pl.BlockSpec((SUBLANES, cwid), next_map(cb + 2)),
            _resident((CONV_TAPS, cwid), lambda i: (0, 0)),
            _resident((aw, d), lambda i: (0, 0)),
            _resident((cwid, d), lambda i: (0, 0)),
            pl.BlockSpec((tm, d), lambda i: (i, 0)),
            _resident((1, d), lambda i: (0, 0)),
            _resident((1, d), lambda i: (0, 0)),
        ],
        out_specs=[pl.BlockSpec((tm, d), lambda i: (i, 0)),
                   pl.BlockSpec((tm, d), lambda i: (i, 0))],
        out_shape=[jax.ShapeDtypeStruct((t, d), F32), jax.ShapeDtypeStruct((t, d), BF16)],
        compiler_params=_cparams(("parallel",)),
    )(oattn, proj, proj, proj, proj, proj, proj, proj, conv_w.astype(F32), wa, wc, x,
      row(ln_g), row(ln_b))


def _memattn_kernel(x1b_ref, x1_ref, wq_ref, kt_ref, v_ref, wo_ref, g_ref, b_ref, rwh_ref, rwl_ref, rb_ref,
                    x2_ref, x2p_ref, te_ref, tg_ref, rk_ref, cnt_ref, carry_ref, *, tm, alpha):
    i = pl.program_id(0)
    d = x1_ref.shape[1]
    hd = d // MEM_HEADS

    @pl.when(i == 0)
    def _():
        carry_ref[...] = jnp.zeros(carry_ref.shape, F32)

    q = _dot(x1b_ref[...], wq_ref[...]).astype(BF16)
    scale = hd ** -0.5
    heads = []
    for hh in range(MEM_HEADS):
        qh = q[:, hh * hd:(hh + 1) * hd]
        vh = v_ref[:, hh * hd:(hh + 1) * hd]
        s = _dot(qh, kt_ref[0, hh * hd:(hh + 1) * hd, :]) * scale
        p = jnp.exp(s - jnp.max(s, axis=1, keepdims=True))
        p = p / jnp.sum(p, axis=1, keepdims=True)
        heads.append(_dot(p.astype(BF16), vh).astype(BF16))
    xa = _dot(jnp.concatenate(heads, axis=1), wo_ref[...])
    x2 = _layer_norm(alpha * x1_ref[...] + xa, g_ref[...], b_ref[...])
    x2_ref[...] = x2

    x2p_ref[...] = _pack_rows(x2[:, :d // 2], x2[:, d // 2:])

    xh = x2.astype(BF16)
    xl = (x2 - xh.astype(F32)).astype(BF16)
    logits = (_dot(xh, rwh_ref[...]) + _dot(xh, rwl_ref[...]) + _dot(xl, rwh_ref[...])
              + rb_ref[...])

    lane = lax.broadcasted_iota(jnp.int32, logits.shape, 1)
    work = logits
    vals, sels, idxs = [], [], []
    for _ in range(TOP_K):
        mk = jnp.max(work, axis=1, keepdims=True)
        idx = jnp.min(jnp.where(work == mk, lane, LANES), axis=1, keepdims=True)
        sel = lane == idx
        vals.append(mk)
        idxs.append(idx)
        sels.append(sel)
        work = jnp.where(sel, -jnp.inf, work)
    exps = [jnp.exp(v - vals[0]) for v in vals]
    denom = exps[0]
    for e in exps[1:]:
        denom = denom + e

    onehot = jnp.zeros(logits.shape, F32)
    for sel in sels:
        onehot = onehot + sel.astype(F32)
    lower = (lax.broadcasted_iota(jnp.int32, (tm, tm), 0)
             > lax.broadcasted_iota(jnp.int32, (tm, tm), 1)).astype(BF16)
    before = _dot(lower, onehot.astype(BF16)) + carry_ref[...]

    te = jnp.zeros(logits.shape, jnp.int32)
    tg = jnp.zeros(logits.shape, F32)
    rk = jnp.zeros(logits.shape, jnp.int32)
    for k in range(TOP_K):
        rank_k = jnp.sum(jnp.where(sels[k], before, 0.0), axis=1, keepdims=True)
        te = jnp.where(lane == k, idxs[k], te)
        tg = jnp.where(lane == k, exps[k] / denom, tg)
        rk = jnp.where(lane == k, rank_k.astype(jnp.int32), rk)
    te_ref[...] = te
    tg_ref[...] = tg
    rk_ref[...] = rk
    carry_ref[...] = carry_ref[...] + jnp.sum(onehot, axis=0, keepdims=True)
    cnt_ref[...] = carry_ref[...]


def _memory_attention(x1b, x1, kv, wq, wo, ln_g, ln_b, router_w, router_b, seq, n_mem, alpha, tm=256):
    t, d = x1.shape
    tm = min(tm, seq)
    ne = router_w.shape[1]
    assert ne <= LANES
    rw = jnp.zeros((d, LANES), F32).at[:, :ne].set(router_w.astype(F32))
    rwh = rw.astype(BF16)
    rwl = (rw - rwh.astype(F32)).astype(BF16)
    rb = jnp.full((1, LANES), NEG_BIG, F32).at[0, :ne].set(router_b.astype(F32))
    row = lambda a: a.reshape(1, -1).astype(F32)
    kt = kv[:, :d].reshape(t // seq, n_mem, d).transpose(0, 2, 1)
    tiles_per_seq = seq // tm
    tile_out = lambda w, dt: (pl.BlockSpec((tm, w), lambda i: (i, 0)), jax.ShapeDtypeStruct((t, w), dt))
    outs = [tile_out(d, F32), tile_out(d // 2, jnp.uint32), tile_out(LANES, jnp.int32),
            tile_out(LANES, F32), tile_out(LANES, jnp.int32),
            (pl.BlockSpec((1, LANES), lambda i: (0, 0)), jax.ShapeDtypeStruct((1, LANES), F32))]
    return pl.pallas_call(
        functools.partial(_memattn_kernel, tm=tm, alpha=alpha),
        grid=(t // tm,),
        in_specs=[
            pl.BlockSpec((tm, d), lambda i: (i, 0)),
            pl.BlockSpec((tm, d), lambda i: (i, 0)),
            _resident((d, d), lambda i: (0, 0)),
            pl.BlockSpec((1, d, n_mem), lambda i: (i // tiles_per_seq, 0, 0)),
            pl.BlockSpec((n_mem, d), lambda i: (i // tiles_per_seq, 1)),
            _resident((d, d), lambda i: (0, 0)),
            _resident((1, d), lambda i: (0, 0)),
            _resident((1, d), lambda i: (0, 0)),
            _resident((d, LANES), lambda i: (0, 0)),
            _resident((d, LANES), lambda i: (0, 0)),
            _resident((1, LANES), lambda i: (0, 0)),
        ],
        out_specs=[o[0] for o in outs],
        out_shape=[o[1] for o in outs],
        scratch_shapes=[pltpu.VMEM((1, LANES), F32)],
        compiler_params=_cparams(("arbitrary",)),
    )(x1b, x1, wq.astype(BF16), kt, kv, wo.astype(BF16), row(ln_g), row(ln_b), rwh, rwl, rb)


def _dispatch_kernel(dest_ref, fill_ref, x2p_ref, xs_ref, zero_ref, sem, zsem, *, rows, tm, n_fill):
    i = pl.program_id(0)

    @pl.when(i == 0)
    def _():
        zero_ref[...] = jnp.zeros(zero_ref.shape, zero_ref.dtype)

        def fill(n):
            start = pl.multiple_of(jnp.maximum(fill_ref[n], 0) * tm, tm)
            return pltpu.make_async_copy(zero_ref, xs_ref.at[pl.ds(start, tm)], zsem)

        for n in range(n_fill):
            @pl.when(fill_ref[n] >= 0)
            def _(n=n):
                fill(n).start()
        for n in range(n_fill):
            @pl.when(fill_ref[n] >= 0)
            def _(n=n):
                fill(n).wait()

    base = i * rows * TOP_K

    def issue(r, carry):
        for k in range(TOP_K):
            dst = dest_ref[base + r * TOP_K + k]
            pltpu.make_async_copy(x2p_ref.at[pl.ds(r, 1)], xs_ref.at[pl.ds(dst, 1)],
                                  sem).start(priority=k % 2)
        return carry

    lax.fori_loop(0, rows, issue, 0)
    for _ in range(TOP_K):
        pltpu.make_async_copy(x2p_ref, xs_ref.at[pl.ds(0, rows)], sem).wait()


def _dispatch(dest, fill_tiles, x2p, n_rows, tm, rows=512):
    t, w = x2p.shape
    rows = min(rows, t)
    n_fill = fill_tiles.shape[0]
    grid_spec = pltpu.PrefetchScalarGridSpec(
        num_scalar_prefetch=2,
        grid=(t // rows,),
        in_specs=[pl.BlockSpec((rows, w), lambda i, d, z: (i, 0))],
        out_specs=pl.BlockSpec(memory_space=pl.ANY),
        scratch_shapes=[pltpu.VMEM((tm, w), x2p.dtype), pltpu.SemaphoreType.DMA,
                        pltpu.SemaphoreType.DMA],
    )
    return pl.pallas_call(
        functools.partial(_dispatch_kernel, rows=rows, tm=tm, n_fill=n_fill),
        grid_spec=grid_spec,
        out_shape=jax.ShapeDtypeStruct((n_rows, w), x2p.dtype),
        compiler_params=_cparams(("arbitrary",)),
    )(dest, fill_tiles, x2p)


def _pack_rows(lo, hi):
    lo = lax.bitcast_convert_type(lo.astype(BF16).astype(F32), jnp.uint32)
    hi = lax.bitcast_convert_type(hi.astype(BF16).astype(F32), jnp.uint32)
    return (lo >> 16) | (hi & jnp.uint32(0xFFFF0000))


def _unpack_rows(words):
    lo = lax.bitcast_convert_type(words << 16, F32)
    hi = lax.bitcast_convert_type(words & jnp.uint32(0xFFFF0000), F32)
    return lo, hi


def _expert_up_kernel(ie_ref, in_ref, irt_ref, ivalid_ref, ifirst_ref, xs_ref, wg_ref, wu_ref,
                      bg_ref, bu_ref, h_ref, wgb_ref, wub_ref):
    w = pl.program_id(0)

    @pl.when(ifirst_ref[w] == 1)
    def _():
        wgb_ref[...] = wg_ref[0].astype(BF16)
        wub_ref[...] = wu_ref[0].astype(BF16)

    @pl.when(ivalid_ref[w] == 1)
    def _():
        lo, hi = (a.astype(BF16) for a in _unpack_rows(xs_ref[...]))
        half = lo.shape[1]
        gate = _dot(lo, wgb_ref[:half, :]) + _dot(hi, wgb_ref[half:, :]) + bg_ref[0]
        up = _dot(lo, wub_ref[:half, :]) + _dot(hi, wub_ref[half:, :]) + bu_ref[0]
        gate = jnp.minimum(gate, SWIGLU_LIMIT)
        up = jnp.clip(up, -SWIGLU_LIMIT, SWIGLU_LIMIT)
        act = gate * jax.nn.sigmoid(SWIGLU_ALPHA * gate) * (up + 1.0)
        h_ref[...] = act.astype(h_ref.dtype)

    @pl.when(ivalid_ref[w] == 0)
    def _():
        h_ref[...] = jnp.zeros(h_ref.shape, h_ref.dtype)


def _expert_down_kernel(ie_ref, in_ref, irt_ref, ivalid_ref, ifirst_ref, h_ref, wd_ref, bd_ref,
                        o_ref, wdb_ref):
    w = pl.program_id(0)

    @pl.when(ifirst_ref[w] == 1)
    def _():
        wdb_ref[...] = wd_ref[0].astype(BF16)

    @pl.when(ivalid_ref[w] == 1)
    def _():
        out = _dot(h_ref[...], wdb_ref[...]) + bd_ref[0]
        half = out.shape[1] // 2
        o_ref[...] = _pack_rows(out[:, :half], out[:, half:])

    @pl.when(ivalid_ref[w] == 0)
    def _():
        o_ref[...] = jnp.zeros(o_ref.shape, o_ref.dtype)


def _work_items(tiles_e, n_tiles_max, n_col):
    ne = tiles_e.shape[0]
    cum = jnp.cumsum(tiles_e)
    total_tiles = cum[-1]
    first_tile = cum - tiles_e
    item_end = cum * n_col
    w = jnp.arange(n_tiles_max * n_col, dtype=jnp.int32)
    e = jnp.minimum(jnp.sum(w[:, None] >= item_end[None, :], axis=1), ne - 1).astype(jnp.int32)
    te = jnp.maximum(tiles_e[e], 1)
    local = w - (item_end[e] - tiles_e[e] * n_col)
    valid = w < total_tiles * n_col
    spare = jnp.maximum(n_tiles_max - total_tiles, 1)
    j = w - total_tiles * n_col
    col = jnp.where(valid, local // te, j // spare)
    rt = jnp.where(valid, first_tile[e] + local % te, total_tiles + j % spare)
    first = jnp.where(valid, (local % te) == 0, False)
    i32 = lambda a: a.astype(jnp.int32)
    return i32(e), i32(col), i32(rt), i32(valid), i32(first)


def _expert_up(items, xs, w_gate, w_up, b_gate, b_up, tm, tn):
    n_rows, half = xs.shape
    ne, d, f = w_gate.shape
    tn = min(tn, f)
    n_items = items[0].shape[0]
    wspec = pl.BlockSpec((1, d, tn), lambda w, ie, ic, irt, iv, ifi: (ie[w], 0, ic[w]))
    bspec = pl.BlockSpec((1, 1, tn), lambda w, ie, ic, irt, iv, ifi: (ie[w], 0, ic[w]))
    grid_spec = pltpu.PrefetchScalarGridSpec(
        num_scalar_prefetch=5,
        grid=(n_items,),
        in_specs=[pl.BlockSpec((tm, half), lambda w, ie, ic, irt, iv, ifi: (irt[w], 0)),
                  wspec, wspec, bspec, bspec],
        out_specs=pl.BlockSpec((tm, tn), lambda w, ie, ic, irt, iv, ifi: (irt[w], ic[w])),
        scratch_shapes=[pltpu.VMEM((d, tn), BF16), pltpu.VMEM((d, tn), BF16)],
    )
    return pl.pallas_call(
        _expert_up_kernel,
        grid_spec=grid_spec,
        out_shape=jax.ShapeDtypeStruct((n_rows, f), BF16),
        compiler_params=_cparams(("arbitrary",)),
    )(*items, xs, w_gate, w_up, b_gate.reshape(ne, 1, f), b_up.reshape(ne, 1, f))


def _expert_down(items, h, w_down, b_down, tm, tn):
    n_rows, f = h.shape
    ne, _, d = w_down.shape
    tn = min(tn, d)
    n_items = items[0].shape[0]
    grid_spec = pltpu.PrefetchScalarGridSpec(
        num_scalar_prefetch=5,
        grid=(n_items,),
        in_specs=[pl.BlockSpec((tm, f), lambda w, ie, ic, irt, iv, ifi: (irt[w], 0)),
                  pl.BlockSpec((1, f, tn), lambda w, ie, ic, irt, iv, ifi: (ie[w], 0, ic[w])),
                  pl.BlockSpec((1, 1, tn), lambda w, ie, ic, irt, iv, ifi: (ie[w], 0, ic[w]))],
        out_specs=pl.BlockSpec((tm, tn // 2), lambda w, ie, ic, irt, iv, ifi: (irt[w], ic[w])),
        scratch_shapes=[pltpu.VMEM((f, tn), BF16)],
    )
    return pl.pallas_call(
        _expert_down_kernel,
        grid_spec=grid_spec,
        out_shape=jax.ShapeDtypeStruct((n_rows, d // 2), jnp.uint32),
        compiler_params=_cparams(("arbitrary",)),
    )(*items, h, w_down, b_down.reshape(ne, 1, d))


def _combine_kernel(dest_ref, os_ref, tg_ref, x2_ref, g_ref, b_ref, o_ref, buf_ref, sem,
                    *, rows, alpha, pack_width):
    base = pl.program_id(0) * rows * TOP_K

    def issue(r, carry):
        for k in range(TOP_K):
            src = dest_ref[base + r * TOP_K + k]
            pltpu.make_async_copy(os_ref.at[pl.ds(src, 1)], buf_ref.at[k, pl.ds(r, 1)],
                                  sem).start(priority=k % 2)
        return carry

    lax.fori_loop(0, rows, issue, 0)
    for k in range(TOP_K):
        pltpu.make_async_copy(os_ref.at[pl.ds(0, rows)], buf_ref.at[k], sem).wait()

    tg = tg_ref[...]
    words = buf_ref.shape[2]
    pieces = []
    for g in range(words // pack_width):
        lo_sum = hi_sum = None
        for k in range(TOP_K):
            lo, hi = _unpack_rows(buf_ref[k, :, g * pack_width:(g + 1) * pack_width])
            gate = tg[:, k:k + 1]
            lo_sum = gate * lo if k == 0 else lo_sum + gate * lo
            hi_sum = gate * hi if k == 0 else hi_sum + gate * hi
        pieces += [lo_sum, hi_sum]
    y = jnp.concatenate(pieces, axis=1)
    o_ref[...] = _layer_norm(alpha * x2_ref[...] + y, g_ref[...], b_ref[...])


def _combine(dest, out_sorted, tg, x2, ln_g, ln_b, alpha, pack_width, rows=256):
    t, d = x2.shape
    rows = min(rows, t)
    row = lambda a: a.reshape(1, -1).astype(F32)
    grid_spec = pltpu.PrefetchScalarGridSpec(
        num_scalar_prefetch=1,
        grid=(t // rows,),
        in_specs=[pl.BlockSpec(memory_space=pl.ANY),
                  pl.BlockSpec((rows, LANES), lambda i, s: (i, 0)),
                  pl.BlockSpec((rows, d), lambda i, s: (i, 0)),
                  pl.BlockSpec((1, d), lambda i, s: (0, 0)),
                  pl.BlockSpec((1, d), lambda i, s: (0, 0))],
        out_specs=pl.BlockSpec((rows, d), lambda i, s: (i, 0)),
        scratch_shapes=[pltpu.VMEM((TOP_K, rows, d // 2), jnp.uint32), pltpu.SemaphoreType.DMA],
    )
    return pl.pallas_call(
        functools.partial(_combine_kernel, rows=rows, alpha=alpha, pack_width=pack_width),
        grid_spec=grid_spec,
        out_shape=jax.ShapeDtypeStruct((t, d), F32),
        compiler_params=_cparams(("arbitrary",)),
    )(dest, out_sorted, tg, x2, row(ln_g), row(ln_b))


def _moe(x2, x2p, te, tg, rk, counts, w_gate, b_gate, w_up, b_up, w_down, b_down, ln_g, ln_b,
         alpha, tm=512, tn_up=1024, tn_down=1024):
    t, d = x2.shape
    ne = w_gate.shape[0]
    n_assign = t * TOP_K
    tm = min(tm, t)
    n_tiles_max = -(-n_assign // tm) + ne
    n_rows = n_tiles_max * tm

    counts = counts[0, :ne].astype(jnp.int32)
    tiles_e = (counts + tm - 1) // tm
    pad_start = (jnp.cumsum(tiles_e) - tiles_e) * tm
    dest = (pad_start[te[:, :TOP_K]] + rk[:, :TOP_K]).reshape(n_assign).astype(jnp.int32)

    cum_tiles = jnp.cumsum(tiles_e)
    last_tile = jnp.where(tiles_e > 0, cum_tiles - 1, -1)
    tail = cum_tiles[-1] + jnp.arange(ne, dtype=jnp.int32)
    fill_tiles = jnp.concatenate([last_tile, jnp.where(tail < n_tiles_max, tail, -1)])

    xs = _dispatch(dest, fill_tiles.astype(jnp.int32), x2p, n_rows, tm)
    f = w_gate.shape[2]
    tn_up, tn_down = min(tn_up, f), min(tn_down, d)
    items_up = _work_items(tiles_e, n_tiles_max, f // tn_up)
    h = _expert_up(items_up, xs, w_gate, w_up, b_gate, b_up, tm, tn_up)
    items_dn = _work_items(tiles_e, n_tiles_max, d // tn_down)
    out_sorted = _expert_down(items_dn, h, w_down, b_down, tm, tn_down)
    return _combine(dest, out_sorted, tg, x2, ln_g, ln_b, alpha, tn_down // 2)


def kernel(x, mem, w_in, conv_w, attn_subln_w, lambda_q1, lambda_k1, lambda_q2, lambda_k2, w_out, ln1_g, ln1_b, mem_wq, mem_wkv, mem_wo, ln2_g, ln2_b, router_w, router_b, w_gate, b_gate, w_up, b_up, w_down, b_down, ln3_g, ln3_b):
    bsz, seq, d = x.shape
    n_mem = mem.shape[1]
    depth = w_in.shape[0]
    alpha = (2.0 * depth) ** 0.25
    slopes = jnp.exp2(-8.0 / N_ATTN_HEADS * jnp.arange(1, N_ATTN_HEADS + 1, dtype=F32))
    xf = x.reshape(bsz * seq, d)
    memf = mem.reshape(bsz * n_mem, d)
    for l in range(depth):
        lambda_init = 0.8 - 0.6 * math.exp(-0.3 * l)
        proj = _projection(xf, w_in[l].astype(BF16), 1024, 1024)
        oattn = _diff_attention(proj.reshape(bsz, seq, -1), slopes, lambda_q1[l], lambda_k1[l],
                                lambda_q2[l], lambda_k2[l], attn_subln_w[l], lambda_init)
        x1, x1b = _out_projection(oattn.reshape(bsz * seq, -1), proj, xf, conv_w[l], w_out[l],
                                  ln1_g[l], ln1_b[l], seq, alpha)
        kv = _projection(memf, mem_wkv[l].astype(BF16), 1024, 1024)
        x2, x2p, te, tg, rk, counts = _memory_attention(
            x1b, x1, kv, mem_wq[l], mem_wo[l], ln2_g[l], ln2_b[l], router_w[l], router_b[l],
            seq, n_mem, alpha)
        xf = _moe(x2, x2p, te, tg, rk, counts, w_gate[l], b_gate[l], w_up[l], b_up[l],
                  w_down[l], b_down[l], ln3_g[l], ln3_b[l], alpha)
    return xf.reshape(bsz, seq, d)
```

```python
import functools
import math

import jax
import jax.numpy as jnp
from jax import lax
from jax.experimental import pallas as pl
from jax.experimental.pallas import tpu as pltpu

N_ATTN_HEADS = 8
DIFF_HEAD_DIM = 64
HEAD_WIDTH = 2 * DIFF_HEAD_DIM
MEM_HEADS = 4
TOP_K = 4
CONV_TAPS = 3
SWIGLU_LIMIT = 7.0
SWIGLU_ALPHA = 1.702
LN_EPS = 1e-5

LANES = 128
SUBLANES = 8
VMEM_LIMIT = 56 * 1024 * 1024
NEG_BIG = -1e30
ALIBI_SPLIT = 16
ALIBI_OFFSET_LANE = 4
QUERY_BLOCK = 256
ONES_ROWS = 16
UNDERFLOW_NATS = 104.0
NORM_SLACK = 1.01
ISSUE_UNROLL = 8

F32 = jnp.float32
BF16 = jnp.bfloat16


def _cparams(sem, vmem=VMEM_LIMIT):
    return pltpu.CompilerParams(dimension_semantics=sem, vmem_limit_bytes=vmem)


def _resident(shape, index_map):
    return pl.BlockSpec(shape, index_map, pipeline_mode=pl.Buffered(1))


def _layer_norm(z, g, b):
    mu = jnp.mean(z, axis=-1, keepdims=True)
    d = z - mu
    var = jnp.mean(d * d, axis=-1, keepdims=True)
    return d * lax.rsqrt(var + LN_EPS) * g + b


def _dot(a, b):
    return jnp.dot(a, b, preferred_element_type=F32)


def _dot_nt(a, b):
    return lax.dot_general(a, b, (((1,), (1,)), ((), ())), preferred_element_type=F32)


def _proj_kernel(x_ref, w_ref, o_ref, xb_ref):
    @pl.when(pl.program_id(1) == 0)
    def _():
        xb_ref[...] = x_ref[...].astype(BF16)

    o_ref[...] = _dot(xb_ref[...], w_ref[...]).astype(o_ref.dtype)


def _projection(x, w, tm, tn):
    m, k = x.shape
    n = w.shape[1]
    tm, tn = min(tm, m), min(tn, n)
    return pl.pallas_call(
        _proj_kernel,
        grid=(m // tm, n // tn),
        in_specs=[pl.BlockSpec((tm, k), lambda i, j: (i, 0)),
                  pl.BlockSpec((k, tn), lambda i, j: (0, j))],
        out_specs=pl.BlockSpec((tm, tn), lambda i, j: (i, j)),
        out_shape=jax.ShapeDtypeStruct((m, n), BF16),
        scratch_shapes=[pltpu.VMEM((tm, k), BF16)],
        compiler_params=_cparams(("parallel", "arbitrary")),
    )(x, w)


def _attn_kernel(slopes_ref, q_ref, k_ref, v_ref, qft_ref, kf_ref, bdt_ref, lq1_ref, lk1_ref, lq2_ref,
                 lk2_ref, sub_ref, o_ref, lhst_ref, vt_ref, st_ref, pt_ref, m_ref, mx_ref, alpha_ref,
                 acct_ref, kn_ref, *, tq, tk, seq, lambda_init):
    h = pl.program_id(1)
    qi = pl.program_id(2)
    slope = slopes_ref[h]

    @pl.when(qi == 0)
    def _():
        half_lane = lax.broadcasted_iota(jnp.int32, (tk, HEAD_WIDTH), 1) < DIFF_HEAD_DIM
        kn1 = jnp.zeros((1, 1), F32)
        kn2 = jnp.zeros((1, 1), F32)
        for c in range(seq // tk):
            vt_ref[c, :HEAD_WIDTH, :] = v_ref[0, c * tk:(c + 1) * tk, :].astype(F32).T.astype(BF16)
            vt_ref[c, HEAD_WIDTH:, :] = jnp.ones((ONES_ROWS, tk), BF16)
            kc = k_ref[0, c * tk:(c + 1) * tk, :].astype(F32)
            sq = kc * kc
            n1 = jnp.sum(jnp.where(half_lane, sq, 0.0), axis=1, keepdims=True)
            n2 = jnp.sum(jnp.where(half_lane, 0.0, sq), axis=1, keepdims=True)
            kn1 = jnp.maximum(kn1, jnp.max(n1, axis=0, keepdims=True))
            kn2 = jnp.maximum(kn2, jnp.max(n2, axis=0, keepdims=True))
        kn_ref[...] = jnp.where(lax.broadcasted_iota(jnp.int32, kn_ref.shape, 1) == 0, kn1, kn2)

    qt = (q_ref[0].astype(F32) * (DIFF_HEAD_DIM ** -0.5)).T
    row = lax.broadcasted_iota(jnp.int32, qt.shape, 0)

    qsq = qt * qt
    qn1 = jnp.max(jnp.sum(jnp.where(row < DIFF_HEAD_DIM, qsq, 0.0), axis=0, keepdims=True),
                  axis=1, keepdims=True)
    qn2 = jnp.max(jnp.sum(jnp.where(row >= DIFF_HEAD_DIM, qsq, 0.0), axis=0, keepdims=True),
                  axis=1, keepdims=True)
    kn = kn_ref[...]
    qk = jnp.sqrt(jnp.maximum(qn1 * kn[:, 0:1], qn2 * kn[:, 1:2]))
    reach = jnp.minimum((UNDERFLOW_NATS + 2.0 * NORM_SLACK * qk) / slope, 4.0 * seq)
    q_lo = (qi * tq).astype(F32)
    lo_f = jnp.floor((q_lo - (tk - 1) - reach) / tk) + 1.0
    hi_f = jnp.ceil((q_lo + (tq - 1) + reach) / tk)
    n_tiles = seq // tk
    per_q = tq // tk
    lo = jnp.clip(lo_f.astype(jnp.int32)[0, 0], 0, qi * per_q)
    hi = jnp.clip(hi_f.astype(jnp.int32)[0, 0], (qi + 1) * per_q, n_tiles)
    if n_tiles > 1:
        short = hi - lo < 2
        lo, hi = (jnp.where(short & (lo > 0), lo - 1, lo), jnp.where(short & (lo == 0), hi + 1, hi))
    qqt = jnp.concatenate([jnp.where(row < DIFF_HEAD_DIM, qt, 0.0),
                           jnp.where(row >= DIFF_HEAD_DIM, qt, 0.0)], axis=1).astype(BF16)
    fft = jnp.concatenate([qft_ref[0], qft_ref[0]], axis=1)
    lhst_ref[0] = jnp.concatenate([qqt, fft], axis=0)
    lhst_ref[1] = jnp.concatenate([qqt, -fft], axis=0)
    lhst_ref[2] = jnp.concatenate([qqt, jnp.zeros_like(fft)], axis=0)

    m_ref[...] = jnp.full(m_ref.shape, NEG_BIG, F32)
    acct_ref[...] = jnp.zeros(acct_ref.shape, F32)
    kf = kf_ref[0]
    klane = lax.broadcasted_iota(jnp.int32, (tk, HEAD_WIDTH), 1)

    blocks = [slice(qb * QUERY_BLOCK, (qb + 1) * QUERY_BLOCK)
              for qb in range(2 * tq // QUERY_BLOCK)]

    def scores(kt):
        start = pl.multiple_of(kt * tk, tk)
        overlap = kt - qi * per_q
        variant = jnp.where(overlap < 0, 0, jnp.where(overlap >= per_q, 1, 2))
        c = slope * (kt * tk - qi * tq).astype(F32)
        feat = jnp.where(klane == ALIBI_OFFSET_LANE, jnp.full(kf.shape, c, F32).astype(BF16), kf)
        k_aug = jnp.concatenate([k_ref[0, pl.ds(start, tk), :], feat], axis=1)
        for cols in blocks:
            st = _dot(k_aug, lhst_ref[variant, :, cols])
            st_ref[:, cols] = st
            mx_ref[:, cols] = jnp.max(st, axis=0, keepdims=True)

        @pl.when(variant == 2)
        def _():
            bias_t = bdt_ref[0, jnp.clip(overlap, 0, per_q - 1)]
            for cols in blocks:
                di = cols.start % tq
                st = st_ref[:, cols] + bias_t[:, di:di + QUERY_BLOCK]
                st_ref[:, cols] = st
                mx_ref[:, cols] = jnp.max(st, axis=0, keepdims=True)

    def probs():
        for cols in blocks:
            st = st_ref[:, cols]
            m_prev = m_ref[:, cols]
            m_new = jnp.maximum(m_prev, mx_ref[:, cols])
            alpha_ref[:, cols] = jnp.exp(m_prev - m_new)
            pt_ref[:, cols] = jnp.exp(st - m_new).astype(BF16)
            m_ref[:, cols] = m_new

    def accumulate(kt):
        vt = vt_ref[kt]
        for cols in blocks:
            acct_ref[:, cols] = alpha_ref[:, cols] * acct_ref[:, cols] + _dot(vt, pt_ref[:, cols])

    def step(j, carry):
        accumulate(j - 2)
        probs()
        scores(j)
        return carry

    scores(lo)
    if n_tiles > 1:
        probs()
        scores(lo + 1)
        lax.fori_loop(lo + 2, hi, step, 0)
        accumulate(hi - 2)
    probs()
    accumulate(hi - 1)

    lam = (jnp.exp(jnp.sum(lq1_ref[...] * lk1_ref[...], keepdims=True))
           - jnp.exp(jnp.sum(lq2_ref[...] * lk2_ref[...], keepdims=True)) + lambda_init)
    acct = acct_ref[...]
    ot = acct[:HEAD_WIDTH, :] / acct[HEAD_WIDTH:HEAD_WIDTH + 1, :]
    ot = ot[:, :tq] - lam * ot[:, tq:]
    ms = jnp.mean(ot * ot, axis=0, keepdims=True)
    yt = ot * lax.rsqrt(ms + LN_EPS) * sub_ref[...] * (1.0 - lambda_init)
    o_ref[0] = yt.T.astype(o_ref.dtype)


def _alibi_features(slopes, t):
    pos = jnp.arange(t, dtype=jnp.int32)
    hi = (pos // ALIBI_SPLIT * ALIBI_SPLIT).astype(F32)[None, :]
    lo = (pos % ALIBI_SPLIT).astype(F32)[None, :]
    sl = slopes[:, None]
    one = jnp.ones((slopes.shape[0], t), F32)
    zero = jnp.zeros_like(one)
    pad = [zero] * (HEAD_WIDTH - 5)
    qf = jnp.stack([one, one, -sl * hi, -sl * lo, one] + pad, axis=-1)
    kf = jnp.stack([sl * hi, sl * lo, one, one, zero] + pad, axis=-1)
    assert ALIBI_OFFSET_LANE == 4
    return qf.astype(BF16), kf.astype(BF16)


def _diff_attention(proj3, slopes, lq1, lk1, lq2, lk2, subln, lambda_init, tq=512, tk=512):
    bsz, seq, _ = proj3.shape
    nh = N_ATTN_HEADS
    tq, tk = min(tq, seq), min(tk, seq)
    assert tq % tk == 0 and seq % tq == 0 and (2 * tq) % QUERY_BLOCK == 0
    qft = _alibi_features(slopes, tq)[0].transpose(0, 2, 1)
    kf = _alibi_features(slopes, tk)[1]
    di = jnp.arange(tq, dtype=jnp.int32)[None, None, :]
    dj = jnp.arange(tk, dtype=jnp.int32)[None, :, None]
    r = jnp.arange(tq // tk, dtype=jnp.int32)[:, None, None]
    bias_t = -slopes[:, None, None, None] * jnp.abs(di - dj - r * tk).astype(F32)[None]
    vec = lambda a: a.reshape(1, -1).astype(F32)
    small = lambda n: pl.BlockSpec((1, n), lambda b, h, i, s: (0, 0))
    grid_spec = pltpu.PrefetchScalarGridSpec(
        num_scalar_prefetch=1,
        grid=(bsz, nh, seq // tq),
        in_specs=[
            pl.BlockSpec((1, tq, HEAD_WIDTH), lambda b, h, i, s: (b, i, h)),
            pl.BlockSpec((1, seq, HEAD_WIDTH), lambda b, h, i, s: (b, 0, nh + h)),
            pl.BlockSpec((1, seq, HEAD_WIDTH), lambda b, h, i, s: (b, 0, 2 * nh + h)),
            pl.BlockSpec((1, HEAD_WIDTH, tq), lambda b, h, i, s: (h, 0, 0)),
            pl.BlockSpec((1, tk, HEAD_WIDTH), lambda b, h, i, s: (h, 0, 0)),
            pl.BlockSpec((1, tq // tk, tk, tq), lambda b, h, i, s: (h, 0, 0, 0)),
            small(DIFF_HEAD_DIM), small(DIFF_HEAD_DIM), small(DIFF_HEAD_DIM), small(DIFF_HEAD_DIM),
            pl.BlockSpec((HEAD_WIDTH, 1), lambda b, h, i, s: (0, 0)),
        ],
        out_specs=pl.BlockSpec((1, tq, HEAD_WIDTH), lambda b, h, i, s: (b, i, h)),
        scratch_shapes=[pltpu.VMEM((3, 2 * HEAD_WIDTH, 2 * tq), BF16),
                        pltpu.VMEM((seq // tk, HEAD_WIDTH + ONES_ROWS, tk), BF16),
                        pltpu.VMEM((tk, 2 * tq), F32),
                        pltpu.VMEM((tk, 2 * tq), BF16),
                        pltpu.VMEM((1, 2 * tq), F32),
                        pltpu.VMEM((1, 2 * tq), F32),
                        pltpu.VMEM((1, 2 * tq), F32),
                        pltpu.VMEM((HEAD_WIDTH + ONES_ROWS, 2 * tq), F32),
                        pltpu.VMEM((1, LANES), F32)],
    )
    return pl.pallas_call(
        functools.partial(_attn_kernel, tq=tq, tk=tk, seq=seq, lambda_init=lambda_init),
        grid_spec=grid_spec,
        out_shape=jax.ShapeDtypeStruct((bsz, seq, nh * HEAD_WIDTH), BF16),
        compiler_params=_cparams(("parallel", "parallel", "arbitrary")),
    )(slopes, proj3, proj3, proj3, qft, kf, bias_t,
      vec(lq1), vec(lk1), vec(lq2), vec(lk2), subln.reshape(-1, 1).astype(F32))


def _outproj_kernel(oa_ref, gb_ref, gc_ref, u_ref, cp_ref, up_ref, cn_ref, un_ref, cw_ref,
                    wa_ref, wc_ref, x_ref, g_ref, b_ref, x1_ref, x1b_ref, *, tm, seq, alpha):
    i = pl.program_id(0)
    cu = gc_ref[...].astype(F32) * u_ref[...].astype(F32)
    row = lax.broadcasted_iota(jnp.int32, cu.shape, 0)
    last = SUBLANES - 1
    prev_edge = cp_ref[last:last + 1, :].astype(F32) * up_ref[last:last + 1, :].astype(F32)
    next_edge = cn_ref[0:1, :].astype(F32) * un_ref[0:1, :].astype(F32)
    prev_edge = jnp.where((i * tm) % seq == 0, 0.0, prev_edge)
    next_edge = jnp.where(((i + 1) * tm) % seq == 0, 0.0, next_edge)
    cu_prev = jnp.where(row == 0, prev_edge, pltpu.roll(cu, 1, 0))
    cu_next = jnp.where(row == tm - 1, next_edge, pltpu.roll(cu, tm - 1, 0))
    cw = cw_ref[...]
    conv = cu_prev * cw[0:1, :] + cu * cw[1:2, :] + cu_next * cw[2:3, :]
    oc = (gb_ref[...].astype(F32) * conv).astype(BF16)
    y = _dot(oa_ref[...], wa_ref[...]) + _dot(oc, wc_ref[...])
    x1 = _layer_norm(alpha * x_ref[...] + y, g_ref[...], b_ref[...])
    x1_ref[...] = x1
    x1b_ref[...] = x1.astype(BF16)


def _out_projection(oattn, proj, x, conv_w, w_out, ln_g, ln_b, seq, alpha, tm=512):
    t, d = x.shape
    aw = oattn.shape[1]
    cwid = d - aw
    assert aw == cwid and proj.shape[1] == 3 * aw + 3 * cwid
    tm = min(tm, seq)
    hb = tm // SUBLANES
    nhb = t // SUBLANES
    cb = 3 * aw // cwid
    prev_map = lambda c: (lambda i: (jnp.maximum(i * hb - 1, 0), c))
    next_map = lambda c: (lambda i: (jnp.minimum((i + 1) * hb, nhb - 1), c))
    wa = w_out[:aw].astype(BF16)
    wc = w_out[aw:].astype(BF16)
    row = lambda a: a.reshape(1, -1).astype(F32)
    return pl.pallas_call(
        functools.partial(_outproj_kernel, tm=tm, seq=seq, alpha=alpha),
        grid=(t // tm,),
        in_specs=[
            pl.BlockSpec((tm, aw), lambda i: (i, 0)),
            pl.BlockSpec((tm, cwid), lambda i: (i, cb)),
            pl.BlockSpec((tm, cwid), lambda i: (i, cb + 1)),
            pl.BlockSpec((tm, cwid), lambda i: (i, cb + 2)),
            pl.BlockSpec((SUBLANES, cwid), prev_map(cb + 1)),
            pl.BlockSpec((SUBLANES, cwid), prev_map(cb + 2)),
            pl.BlockSpec((SUBLANES, cwid), next_map(cb + 1)),
            pl.BlockSpec((SUBLANES, cwid), next_map(cb + 2)),
            _resident((CONV_TAPS, cwid), lambda i: (0, 0)),
            _resident((aw, d), lambda i: (0, 0)),
            _resident((cwid, d), lambda i: (0, 0)),
            pl.BlockSpec((tm, d), lambda i: (i, 0)),
            _resident((1, d), lambda i: (0, 0)),
            _resident((1, d), lambda i: (0, 0)),
        ],
        out_specs=[pl.BlockSpec((tm, d), lambda i: (i, 0)),
                   pl.BlockSpec((tm, d), lambda i: (i, 0))],
        out_shape=[jax.ShapeDtypeStruct((t, d), F32), jax.ShapeDtypeStruct((t, d), BF16)],
        compiler_params=_cparams(("parallel",)),
    )(oattn, proj, proj, proj, proj, proj, proj, proj, conv_w.astype(F32), wa, wc, x,
      row(ln_g), row(ln_b))


def _memattn_kernel(x1b_ref, x1_ref, wq_ref, kt_ref, v_ref, wo_ref, g_ref, b_ref, rw_ref, rb_ref,
                    x2_ref, x2p_ref, te_ref, tg_ref, rk_ref, cnt_ref, carry_ref, *, tm, alpha):
    i = pl.program_id(0)
    d = x1_ref.shape[1]
    hd = d // MEM_HEADS

    @pl.when(i == 0)
    def _():
        carry_ref[...] = jnp.zeros(carry_ref.shape, F32)

    q = _dot(x1b_ref[...], wq_ref[...]).astype(BF16)
    scale = hd ** -0.5
    heads = []
    for hh in range(MEM_HEADS):
        cols = slice(hh * hd, (hh + 1) * hd)
        s = _dot(q[:, cols], kt_ref[0, cols, :]) * scale
        p = jnp.exp(s - jnp.max(s, axis=1, keepdims=True))
        p = p / jnp.sum(p, axis=1, keepdims=True)
        heads.append(_dot(p.astype(BF16), v_ref[:, cols]).astype(BF16))
    xa = _dot(jnp.concatenate(heads, axis=1), wo_ref[...])
    x2 = _layer_norm(alpha * x1_ref[...] + xa, g_ref[...], b_ref[...])
    x2_ref[...] = x2

    _store_row_tiles(x2p_ref, _pack_rows(x2[:, :d // 2], x2[:, d // 2:]))

    xh = x2.astype(BF16)
    xl = (x2 - xh.astype(F32)).astype(BF16)
    both = _dot(xh, rw_ref[...])
    logits = both[:, :LANES] + both[:, LANES:] + _dot(xl, rw_ref[:, :LANES]) + rb_ref[...]

    lane = lax.broadcasted_iota(jnp.int32, logits.shape, 1)
    work = logits
    vals, sels, idxs = [], [], []
    for _ in range(TOP_K):
        mk = jnp.max(work, axis=1, keepdims=True)
        idx = jnp.min(jnp.where(work == mk, lane, LANES), axis=1, keepdims=True)
        sel = lane == idx
        vals.append(mk)
        idxs.append(idx)
        sels.append(sel)
        work = jnp.where(sel, -jnp.inf, work)
    exps = [jnp.exp(v - vals[0]) for v in vals]
    denom = exps[0]
    for e in exps[1:]:
        denom = denom + e

    onehot = jnp.zeros(logits.shape, F32)
    for sel in sels:
        onehot = onehot + sel.astype(F32)
    lower = (lax.broadcasted_iota(jnp.int32, (tm, tm), 0)
             > lax.broadcasted_iota(jnp.int32, (tm, tm), 1)).astype(BF16)
    before = _dot(lower, onehot.astype(BF16)) + carry_ref[...]

    te = jnp.zeros(logits.shape, jnp.int32)
    tg = jnp.zeros(logits.shape, F32)
    rk = jnp.zeros(logits.shape, jnp.int32)
    for k in range(TOP_K):
        rank_k = jnp.sum(jnp.where(sels[k], before, 0.0), axis=1, keepdims=True)
        te = jnp.where(lane == k, idxs[k], te)
        tg = jnp.where(lane == k, exps[k] / denom, tg)
        rk = jnp.where(lane == k, rank_k.astype(jnp.int32), rk)
    te_ref[...] = te
    tg_ref[...] = tg
    rk_ref[...] = rk
    carry_ref[...] = carry_ref[...] + jnp.sum(onehot, axis=0, keepdims=True)
    cnt_ref[...] = carry_ref[...]


def _memory_attention(x1b, x1, kv, wq, wo, ln_g, ln_b, router_w, router_b, seq, n_mem, alpha, tm=256):
    t, d = x1.shape
    tm = min(tm, seq)
    ne = router_w.shape[1]
    assert ne <= LANES
    rw = jnp.zeros((d, LANES), F32).at[:, :ne].set(router_w.astype(F32))
    rwh = rw.astype(BF16)
    rwl = (rw - rwh.astype(F32)).astype(BF16)
    rb = jnp.full((1, LANES), NEG_BIG, F32).at[0, :ne].set(router_b.astype(F32))
    row = lambda a: a.reshape(1, -1).astype(F32)
    kt = kv[:, :d].reshape(t // seq, n_mem, d).transpose(0, 2, 1)
    tiles_per_seq = seq // tm
    tile_out = lambda w, dt: (pl.BlockSpec((tm, w), lambda i: (i, 0)), jax.ShapeDtypeStruct((t, w), dt))
    rh = d // 2 // LANES
    x2p_out = (pl.BlockSpec((tm * rh, LANES), lambda i: (i, 0)),
               jax.ShapeDtypeStruct((t * rh, LANES), jnp.uint32))
    outs = [tile_out(d, F32), x2p_out, tile_out(LANES, jnp.int32),
            tile_out(LANES, F32), tile_out(LANES, jnp.int32),
            (pl.BlockSpec((1, LANES), lambda i: (0, 0)), jax.ShapeDtypeStruct((1, LANES), F32))]
    return pl.pallas_call(
        functools.partial(_memattn_kernel, tm=tm, alpha=alpha),
        grid=(t // tm,),
        in_specs=[
            pl.BlockSpec((tm, d), lambda i: (i, 0)),
            pl.BlockSpec((tm, d), lambda i: (i, 0)),
            _resident((d, d), lambda i: (0, 0)),
            pl.BlockSpec((1, d, n_mem), lambda i: (i // tiles_per_seq, 0, 0)),
            pl.BlockSpec((n_mem, d), lambda i: (i // tiles_per_seq, 1)),
            _resident((d, d), lambda i: (0, 0)),
            _resident((1, d), lambda i: (0, 0)),
            _resident((1, d), lambda i: (0, 0)),
            _resident((d, 2 * LANES), lambda i: (0, 0)),
            _resident((1, LANES), lambda i: (0, 0)),
        ],
        out_specs=[o[0] for o in outs],
        out_shape=[o[1] for o in outs],
        scratch_shapes=[pltpu.VMEM((1, LANES), F32)],
        compiler_params=_cparams(("arbitrary",)),
    )(x1b, x1, wq.astype(BF16), kt, kv, wo.astype(BF16), row(ln_g), row(ln_b),
      jnp.concatenate([rwh, rwl], axis=1), rb)


def _dispatch_kernel(dest_ref, fill_ref, x2p_ref, xs_ref, zero_ref, sem, zsem,
                     *, rows, tm, n_fill, rh):
    i = pl.program_id(0)

    @pl.when(i == 0)
    def _():
        zero_ref[...] = jnp.zeros(zero_ref.shape, zero_ref.dtype)

        def fill(n):
            start = pl.multiple_of(jnp.maximum(fill_ref[n], 0) * (tm * rh), tm * rh)
            return pltpu.make_async_copy(zero_ref, xs_ref.at[pl.ds(start, tm * rh)], zsem)

        for n in range(n_fill):
            @pl.when(fill_ref[n] >= 0)
            def _(n=n):
                fill(n).start()
        for n in range(n_fill):
            @pl.when(fill_ref[n] >= 0)
            def _(n=n):
                fill(n).wait()

    base = i * rows * TOP_K

    def issue(r, carry):
        src = pl.multiple_of(r * rh, rh)
        for k in range(TOP_K):
            dst = pl.multiple_of(dest_ref[base + r * TOP_K + k] * rh, rh)
            pltpu.make_async_copy(x2p_ref.at[pl.ds(src, rh)], xs_ref.at[pl.ds(dst, rh)],
                                  sem).start(priority=k % 2)
        return carry

    lax.fori_loop(0, rows, issue, 0, unroll=ISSUE_UNROLL)
    for _ in range(TOP_K):
        pltpu.make_async_copy(x2p_ref, xs_ref.at[pl.ds(0, rows * rh)], sem).wait()


def _dispatch(dest, fill_tiles, x2p, n_rows, tm, rh, rows=512):
    t = x2p.shape[0] // rh
    rows = min(rows, t)
    n_fill = fill_tiles.shape[0]
    grid_spec = pltpu.PrefetchScalarGridSpec(
        num_scalar_prefetch=2,
        grid=(t // rows,),
        in_specs=[pl.BlockSpec((rows * rh, LANES), lambda i, d, z: (i, 0))],
        out_specs=pl.BlockSpec(memory_space=pl.ANY),
        scratch_shapes=[pltpu.VMEM((tm * rh, LANES), x2p.dtype), pltpu.SemaphoreType.DMA,
                        pltpu.SemaphoreType.DMA],
    )
    return pl.pallas_call(
        functools.partial(_dispatch_kernel, rows=rows, tm=tm, n_fill=n_fill, rh=rh),
        grid_spec=grid_spec,
        out_shape=jax.ShapeDtypeStruct((n_rows * rh, LANES), x2p.dtype),
        compiler_params=_cparams(("arbitrary",)),
    )(dest, fill_tiles, x2p)


def _pack_rows(lo, hi):
    lo = lax.bitcast_convert_type(lo.astype(BF16).astype(F32), jnp.uint32)
    hi = lax.bitcast_convert_type(hi.astype(BF16).astype(F32), jnp.uint32)
    return (lo >> 16) | (hi & jnp.uint32(0xFFFF0000))


def _store_row_tiles(ref, words):
    rows, width = words.shape
    chunks = width // LANES
    for c in range(chunks):
        ref[pl.ds(c, rows, stride=chunks), :] = words[:, c * LANES:(c + 1) * LANES]


def _load_row_tiles(ref, chunks):
    rows = ref.shape[0] // chunks
    return jnp.concatenate([ref[pl.ds(c, rows, stride=chunks), :] for c in range(chunks)], axis=1)


def _unpack_rows(words):
    lo = lax.bitcast_convert_type(words << 16, F32)
    hi = lax.bitcast_convert_type(words & jnp.uint32(0xFFFF0000), F32)
    return lo, hi


def _expert_up_kernel(ie_ref, in_ref, irt_ref, ivalid_ref, ifirst_ref, xs_ref, wg_ref, wu_ref,
                      bg_ref, bu_ref, h_ref, wgb_ref, wub_ref):
    w = pl.program_id(0)

    @pl.when(ifirst_ref[w] == 1)
    def _():
        wgb_ref[...] = wg_ref[0].astype(BF16)
        wub_ref[...] = wu_ref[0].astype(BF16)

    @pl.when(ivalid_ref[w] == 1)
    def _():
        rh = wgb_ref.shape[0] // 2 // LANES
        lo, hi = (a.astype(BF16) for a in _unpack_rows(_load_row_tiles(xs_ref, rh)))
        half = lo.shape[1]
        gate = _dot(lo, wgb_ref[:half, :]) + _dot(hi, wgb_ref[half:, :]) + bg_ref[0]
        up = _dot(lo, wub_ref[:half, :]) + _dot(hi, wub_ref[half:, :]) + bu_ref[0]
        gate = jnp.minimum(gate, SWIGLU_LIMIT)
        up = jnp.clip(up, -SWIGLU_LIMIT, SWIGLU_LIMIT)
        act = gate * jax.nn.sigmoid(SWIGLU_ALPHA * gate) * (up + 1.0)
        h_ref[...] = act.astype(h_ref.dtype)

    @pl.when(ivalid_ref[w] == 0)
    def _():
        h_ref[...] = jnp.zeros(h_ref.shape, h_ref.dtype)


def _expert_down_kernel(ie_ref, in_ref, irt_ref, ivalid_ref, ifirst_ref, h_ref, wd_ref, bd_ref,
                        o_ref, wdb_ref):
    w = pl.program_id(0)

    @pl.when(ifirst_ref[w] == 1)
    def _():
        wdb_ref[...] = wd_ref[0].astype(BF16)

    @pl.when(ivalid_ref[w] == 1)
    def _():
        out = _dot(h_ref[...], wdb_ref[...]) + bd_ref[0]
        half = out.shape[1] // 2
        o_ref[...] = _pack_rows(out[:, :half], out[:, half:])

    @pl.when(ivalid_ref[w] == 0)
    def _():
        o_ref[...] = jnp.zeros(o_ref.shape, o_ref.dtype)


def _work_items(tiles_e, n_tiles_max, n_col):
    ne = tiles_e.shape[0]
    cum = jnp.cumsum(tiles_e)
    total_tiles = cum[-1]
    first_tile = cum - tiles_e
    item_end = cum * n_col
    w = jnp.arange(n_tiles_max * n_col, dtype=jnp.int32)
    e = jnp.minimum(jnp.sum(w[:, None] >= item_end[None, :], axis=1), ne - 1).astype(jnp.int32)
    te = jnp.maximum(tiles_e[e], 1)
    local = w - (item_end[e] - tiles_e[e] * n_col)
    valid = w < total_tiles * n_col
    spare = jnp.maximum(n_tiles_max - total_tiles, 1)
    j = w - total_tiles * n_col
    col = jnp.where(valid, local // te, j // spare)
    rt = jnp.where(valid, first_tile[e] + local % te, total_tiles + j % spare)
    first = jnp.where(valid, (local % te) == 0, False)
    i32 = lambda a: a.astype(jnp.int32)
    return i32(e), i32(col), i32(rt), i32(valid), i32(first)


def _expert_up(items, xs, w_gate, w_up, b_gate, b_up, tm, tn):
    ne, d, f = w_gate.shape
    rh = d // 2 // LANES
    n_rows = xs.shape[0] // rh
    tn = min(tn, f)
    n_items = items[0].shape[0]
    wspec = pl.BlockSpec((1, d, tn), lambda w, ie, ic, irt, iv, ifi: (ie[w], 0, ic[w]))
    bspec = pl.BlockSpec((1, 1, tn), lambda w, ie, ic, irt, iv, ifi: (ie[w], 0, ic[w]))
    grid_spec = pltpu.PrefetchScalarGridSpec(
        num_scalar_prefetch=5,
        grid=(n_items,),
        in_specs=[pl.BlockSpec((tm * rh, LANES), lambda w, ie, ic, irt, iv, ifi: (irt[w], 0)),
                  wspec, wspec, bspec, bspec],
        out_specs=pl.BlockSpec((tm, tn), lambda w, ie, ic, irt, iv, ifi: (irt[w], ic[w])),
        scratch_shapes=[pltpu.VMEM((d, tn), BF16), pltpu.VMEM((d, tn), BF16)],
    )
    return pl.pallas_call(
        _expert_up_kernel,
        grid_spec=grid_spec,
        out_shape=jax.ShapeDtypeStruct((n_rows, f), BF16),
        compiler_params=_cparams(("arbitrary",)),
    )(*items, xs, w_gate, w_up, b_gate.reshape(ne, 1, f), b_up.reshape(ne, 1, f))


def _expert_down(items, h, w_down, b_down, tm, tn):
    n_rows, f = h.shape
    ne, _, d = w_down.shape
    tn = min(tn, d)
    n_items = items[0].shape[0]
    grid_spec = pltpu.PrefetchScalarGridSpec(
        num_scalar_prefetch=5,
        grid=(n_items,),
        in_specs=[pl.BlockSpec((tm, f), lambda w, ie, ic, irt, iv, ifi: (irt[w], 0)),
                  pl.BlockSpec((1, f, tn), lambda w, ie, ic, irt, iv, ifi: (ie[w], 0, ic[w])),
                  pl.BlockSpec((1, 1, tn), lambda w, ie, ic, irt, iv, ifi: (ie[w], 0, ic[w]))],
        out_specs=pl.BlockSpec((tm, tn // 2), lambda w, ie, ic, irt, iv, ifi: (irt[w], ic[w])),
        scratch_shapes=[pltpu.VMEM((f, tn), BF16)],
    )
    return pl.pallas_call(
        _expert_down_kernel,
        grid_spec=grid_spec,
        out_shape=jax.ShapeDtypeStruct((n_rows, d // 2), jnp.uint32),
        compiler_params=_cparams(("arbitrary",)),
    )(*items, h, w_down, b_down.reshape(ne, 1, d))


def _combine_kernel(dest_ref, os_ref, tg_ref, x2_ref, g_ref, b_ref, o_ref, buf_ref, sem,
                    *, rows, alpha, pack_width, n_steps):
    i = pl.program_id(0)
    slot = i % 2

    def gather(step, into):
        base = step * rows * TOP_K

        def issue(r, carry):
            for k in range(TOP_K):
                src = dest_ref[base + r * TOP_K + k]
                pltpu.make_async_copy(os_ref.at[pl.ds(src, 1)], buf_ref.at[into, k, pl.ds(r, 1)],
                                      sem.at[into]).start(priority=k % 2)
            return carry

        lax.fori_loop(0, rows, issue, 0, unroll=ISSUE_UNROLL)

    @pl.when(i == 0)
    def _():
        gather(0, 0)

    @pl.when(i + 1 < n_steps)
    def _():
        gather(i + 1, 1 - slot)

    for k in range(TOP_K):
        pltpu.make_async_copy(os_ref.at[pl.ds(0, rows)], buf_ref.at[slot, k], sem.at[slot]).wait()

    tg = tg_ref[...]
    words = buf_ref.shape[3]
    pieces = []
    for g in range(words // pack_width):
        lo_sum = hi_sum = None
        for k in range(TOP_K):
            lo, hi = _unpack_rows(buf_ref[slot, k, :, g * pack_width:(g + 1) * pack_width])
            gate = tg[:, k:k + 1]
            lo_sum = gate * lo if k == 0 else lo_sum + gate * lo
            hi_sum = gate * hi if k == 0 else hi_sum + gate * hi
        pieces += [lo_sum, hi_sum]
    y = jnp.concatenate(pieces, axis=1)
    o_ref[...] = _layer_norm(alpha * x2_ref[...] + y, g_ref[...], b_ref[...])


def _combine(dest, out_sorted, tg, x2, ln_g, ln_b, alpha, pack_width, rows=256):
    t, d = x2.shape
    rows = min(rows, t)
    row = lambda a: a.reshape(1, -1).astype(F32)
    grid_spec = pltpu.PrefetchScalarGridSpec(
        num_scalar_prefetch=1,
        grid=(t // rows,),
        in_specs=[pl.BlockSpec(memory_space=pl.ANY),
                  pl.BlockSpec((rows, LANES), lambda i, s: (i, 0)),
                  pl.BlockSpec((rows, d), lambda i, s: (i, 0)),
                  pl.BlockSpec((1, d), lambda i, s: (0, 0)),
                  pl.BlockSpec((1, d), lambda i, s: (0, 0))],
        out_specs=pl.BlockSpec((rows, d), lambda i, s: (i, 0)),
        scratch_shapes=[pltpu.VMEM((2, TOP_K, rows, d // 2), jnp.uint32),
                        pltpu.SemaphoreType.DMA((2,))],
    )
    return pl.pallas_call(
        functools.partial(_combine_kernel, rows=rows, alpha=alpha, pack_width=pack_width,
                          n_steps=t // rows),
        grid_spec=grid_spec,
        out_shape=jax.ShapeDtypeStruct((t, d), F32),
        compiler_params=_cparams(("arbitrary",)),
    )(dest, out_sorted, tg, x2, row(ln_g), row(ln_b))


def _moe(x2, x2p, te, tg, rk, counts, w_gate, b_gate, w_up, b_up, w_down, b_down, ln_g, ln_b,
         alpha, tm=512, tn_up=1024, tn_down=1024):
    t, d = x2.shape
    ne = w_gate.shape[0]
    n_assign = t * TOP_K
    tm = min(tm, t)
    n_tiles_max = -(-n_assign // tm) + ne
    n_rows = n_tiles_max * tm

    counts = counts[0, :ne].astype(jnp.int32)
    tiles_e = (counts + tm - 1) // tm
    pad_start = (jnp.cumsum(tiles_e) - tiles_e) * tm
    dest = (pad_start[te[:, :TOP_K]] + rk[:, :TOP_K]).reshape(n_assign).astype(jnp.int32)

    cum_tiles = jnp.cumsum(tiles_e)
    last_tile = jnp.where(tiles_e > 0, cum_tiles - 1, -1)
    tail = cum_tiles[-1] + jnp.arange(ne, dtype=jnp.int32)
    fill_tiles = jnp.concatenate([last_tile, jnp.where(tail < n_tiles_max, tail, -1)])

    xs = _dispatch(dest, fill_tiles.astype(jnp.int32), x2p, n_rows, tm, d // 2 // LANES)
    f = w_gate.shape[2]
    tn_up, tn_down = min(tn_up, f), min(tn_down, d)
    items_up = _work_items(tiles_e, n_tiles_max, f // tn_up)
    h = _expert_up(items_up, xs, w_gate, w_up, b_gate, b_up, tm, tn_up)
    items_dn = _work_items(tiles_e, n_tiles_max, d // tn_down)
    out_sorted = _expert_down(items_dn, h, w_down, b_down, tm, tn_down)
    return _combine(dest, out_sorted, tg, x2, ln_g, ln_b, alpha, tn_down // 2)


def kernel(x, mem, w_in, conv_w, attn_subln_w, lambda_q1, lambda_k1, lambda_q2, lambda_k2, w_out, ln1_g, ln1_b, mem_wq, mem_wkv, mem_wo, ln2_g, ln2_b, router_w, router_b, w_gate, b_gate, w_up, b_up, w_down, b_down, ln3_g, ln3_b):
    bsz, seq, d = x.shape
    n_mem = mem.shape[1]
    depth = w_in.shape[0]
    alpha = (2.0 * depth) ** 0.25
    slopes = jnp.exp2(-8.0 / N_ATTN_HEADS * jnp.arange(1, N_ATTN_HEADS + 1, dtype=F32))
    xf = x.reshape(bsz * seq, d)
    memf = mem.reshape(bsz * n_mem, d)
    for l in range(depth):
        lambda_init = 0.8 - 0.6 * math.exp(-0.3 * l)
        proj = _projection(xf, w_in[l].astype(BF16), 1024, 1024)
        oattn = _diff_attention(proj.reshape(bsz, seq, -1), slopes, lambda_q1[l], lambda_k1[l],
                                lambda_q2[l], lambda_k2[l], attn_subln_w[l], lambda_init)
        x1, x1b = _out_projection(oattn.reshape(bsz * seq, -1), proj, xf, conv_w[l], w_out[l],
                                  ln1_g[l], ln1_b[l], seq, alpha)
        kv = _projection(memf, mem_wkv[l].astype(BF16), 1024, 1024)
        x2, x2p, te, tg, rk, counts = _memory_attention(
            x1b, x1, kv, mem_wq[l], mem_wo[l], ln2_g[l], ln2_b[l], router_w[l], router_b[l],
            seq, n_mem, alpha)
        xf = _moe(x2, x2p, te, tg, rk, counts, w_gate[l], b_gate[l], w_up[l], b_up[l],
                  w_down[l], b_down[l], ln3_g[l], ln3_b[l], alpha)
    return xf.reshape(bsz, seq, d)
```

```python
import functools
import math

import jax
import jax.numpy as jnp
from jax import lax
from jax.experimental import pallas as pl
from jax.experimental.pallas import tpu as pltpu

N_ATTN_HEADS = 8
DIFF_HEAD_DIM = 64
HEAD_WIDTH = 2 * DIFF_HEAD_DIM
MEM_HEADS = 4
TOP_K = 4
CONV_TAPS = 3
SWIGLU_LIMIT = 7.0
SWIGLU_ALPHA = 1.702
LN_EPS = 1e-5

LANES = 128
SUBLANES = 8
VMEM_LIMIT = 56 * 1024 * 1024
NEG_BIG = -1e30
ALIBI_SPLIT = 16
ALIBI_OFFSET_LANE = 4
QUERY_BLOCK = 256
ONES_ROWS = 16
UNDERFLOW_NATS = 104.0
NORM_SLACK = 1.01
ISSUE_UNROLL = 8

F32 = jnp.float32
BF16 = jnp.bfloat16


def _cparams(sem, vmem=VMEM_LIMIT):
    return pltpu.CompilerParams(dimension_semantics=sem, vmem_limit_bytes=vmem)


def _resident(shape, index_map):
    return pl.BlockSpec(shape, index_map, pipeline_mode=pl.Buffered(1))


def _layer_norm(z, g, b):
    mu = jnp.mean(z, axis=-1, keepdims=True)
    d = z - mu
    var = jnp.mean(d * d, axis=-1, keepdims=True)
    return d * lax.rsqrt(var + LN_EPS) * g + b


def _dot(a, b):
    return jnp.dot(a, b, preferred_element_type=F32)


def _dot_nt(a, b):
    return lax.dot_general(a, b, (((1,), (1,)), ((), ())), preferred_element_type=F32)


def _proj_kernel(x_ref, w_ref, o_ref, xb_ref):
    @pl.when(pl.program_id(1) == 0)
    def _():
        xb_ref[...] = x_ref[...].astype(BF16)

    o_ref[...] = _dot(xb_ref[...], w_ref[...]).astype(o_ref.dtype)


def _projection(x, w, tm, tn):
    m, k = x.shape
    n = w.shape[1]
    tm, tn = min(tm, m), min(tn, n)
    return pl.pallas_call(
        _proj_kernel,
        grid=(m // tm, n // tn),
        in_specs=[pl.BlockSpec((tm, k), lambda i, j: (i, 0)),
                  pl.BlockSpec((k, tn), lambda i, j: (0, j))],
        out_specs=pl.BlockSpec((tm, tn), lambda i, j: (i, j)),
        out_shape=jax.ShapeDtypeStruct((m, n), BF16),
        scratch_shapes=[pltpu.VMEM((tm, k), BF16)],
        compiler_params=_cparams(("parallel", "arbitrary")),
    )(x, w)


def _attn_kernel(slopes_ref, q_ref, k_ref, v_ref, qft_ref, kf_ref, bdt_ref, lq1_ref, lk1_ref, lq2_ref,
                 lk2_ref, sub_ref, o_ref, lhst_ref, vt_ref, st_ref, pt_ref, m_ref, mx_ref, alpha_ref,
                 acct_ref, kn_ref, *, tq, tk, seq, lambda_init):
    h = pl.program_id(1)
    qi = pl.program_id(2)
    slope = slopes_ref[h]

    @pl.when(qi == 0)
    def _():
        half_lane = lax.broadcasted_iota(jnp.int32, (tk, HEAD_WIDTH), 1) < DIFF_HEAD_DIM
        kn1 = jnp.zeros((1, 1), F32)
        kn2 = jnp.zeros((1, 1), F32)
        for c in range(seq // tk):
            vt_ref[c, :HEAD_WIDTH, :] = v_ref[0, c * tk:(c + 1) * tk, :].astype(F32).T.astype(BF16)
            vt_ref[c, HEAD_WIDTH:, :] = jnp.ones((ONES_ROWS, tk), BF16)
            kc = k_ref[0, c * tk:(c + 1) * tk, :].astype(F32)
            sq = kc * kc
            n1 = jnp.sum(jnp.where(half_lane, sq, 0.0), axis=1, keepdims=True)
            n2 = jnp.sum(jnp.where(half_lane, 0.0, sq), axis=1, keepdims=True)
            kn1 = jnp.maximum(kn1, jnp.max(n1, axis=0, keepdims=True))
            kn2 = jnp.maximum(kn2, jnp.max(n2, axis=0, keepdims=True))
        kn_ref[...] = jnp.where(lax.broadcasted_iota(jnp.int32, kn_ref.shape, 1) == 0, kn1, kn2)

    qt = (q_ref[0].astype(F32) * (DIFF_HEAD_DIM ** -0.5)).T
    row = lax.broadcasted_iota(jnp.int32, qt.shape, 0)

    qsq = qt * qt
    qn1 = jnp.max(jnp.sum(jnp.where(row < DIFF_HEAD_DIM, qsq, 0.0), axis=0, keepdims=True),
                  axis=1, keepdims=True)
    qn2 = jnp.max(jnp.sum(jnp.where(row >= DIFF_HEAD_DIM, qsq, 0.0), axis=0, keepdims=True),
                  axis=1, keepdims=True)
    kn = kn_ref[...]
    qk = jnp.sqrt(jnp.maximum(qn1 * kn[:, 0:1], qn2 * kn[:, 1:2]))
    reach = jnp.minimum((UNDERFLOW_NATS + 2.0 * NORM_SLACK * qk) / slope, 4.0 * seq)
    q_lo = (qi * tq).astype(F32)
    lo_f = jnp.floor((q_lo - (tk - 1) - reach) / tk) + 1.0
    hi_f = jnp.ceil((q_lo + (tq - 1) + reach) / tk)
    n_tiles = seq // tk
    per_q = tq // tk
    lo = jnp.clip(lo_f.astype(jnp.int32)[0, 0], 0, qi * per_q)
    hi = jnp.clip(hi_f.astype(jnp.int32)[0, 0], (qi + 1) * per_q, n_tiles)
    if n_tiles > 1:
        short = hi - lo < 2
        lo, hi = (jnp.where(short & (lo > 0), lo - 1, lo), jnp.where(short & (lo == 0), hi + 1, hi))
    qqt = jnp.concatenate([jnp.where(row < DIFF_HEAD_DIM, qt, 0.0),
                           jnp.where(row >= DIFF_HEAD_DIM, qt, 0.0)], axis=1).astype(BF16)
    fft = jnp.concatenate([qft_ref[0], qft_ref[0]], axis=1)
    lhst_ref[0] = jnp.concatenate([qqt, fft], axis=0)
    lhst_ref[1] = jnp.concatenate([qqt, -fft], axis=0)
    lhst_ref[2] = jnp.concatenate([qqt, jnp.zeros_like(fft)], axis=0)

    m_ref[...] = jnp.full(m_ref.shape, NEG_BIG, F32)
    acct_ref[...] = jnp.zeros(acct_ref.shape, F32)
    kf = kf_ref[0]
    klane = lax.broadcasted_iota(jnp.int32, (tk, HEAD_WIDTH), 1)

    blocks = [slice(qb * QUERY_BLOCK, (qb + 1) * QUERY_BLOCK)
              for qb in range(2 * tq // QUERY_BLOCK)]

    def scores(kt):
        start = pl.multiple_of(kt * tk, tk)
        overlap = kt - qi * per_q
        variant = jnp.where(overlap < 0, 0, jnp.where(overlap >= per_q, 1, 2))
        c = slope * (kt * tk - qi * tq).astype(F32)
        feat = jnp.where(klane == ALIBI_OFFSET_LANE, jnp.full(kf.shape, c, F32).astype(BF16), kf)
        k_aug = jnp.concatenate([k_ref[0, pl.ds(start, tk), :], feat], axis=1)
        for cols in blocks:
            st = _dot(k_aug, lhst_ref[variant, :, cols])
            st_ref[:, cols] = st
            mx_ref[:, cols] = jnp.max(st, axis=0, keepdims=True)

        @pl.when(variant == 2)
        def _():
            bias_t = bdt_ref[0, jnp.clip(overlap, 0, per_q - 1)]
            for cols in blocks:
                di = cols.start % tq
                st = st_ref[:, cols] + bias_t[:, di:di + QUERY_BLOCK]
                st_ref[:, cols] = st
                mx_ref[:, cols] = jnp.max(st, axis=0, keepdims=True)

    def probs():
        for cols in blocks:
            st = st_ref[:, cols]
            m_prev = m_ref[:, cols]
            m_new = jnp.maximum(m_prev, mx_ref[:, cols])
            alpha_ref[:, cols] = jnp.exp(m_prev - m_new)
            pt_ref[:, cols] = jnp.exp(st - m_new).astype(BF16)
            m_ref[:, cols] = m_new

    def accumulate(kt):
        vt = vt_ref[kt]
        for cols in blocks:
            acct_ref[:, cols] = alpha_ref[:, cols] * acct_ref[:, cols] + _dot(vt, pt_ref[:, cols])

    def step(j, carry):
        accumulate(j - 2)
        probs()
        scores(j)
        return carry

    scores(lo)
    if n_tiles > 1:
        probs()
        scores(lo + 1)
        lax.fori_loop(lo + 2, hi, step, 0)
        accumulate(hi - 2)
    probs()
    accumulate(hi - 1)

    lam = (jnp.exp(jnp.sum(lq1_ref[...] * lk1_ref[...], keepdims=True))
           - jnp.exp(jnp.sum(lq2_ref[...] * lk2_ref[...], keepdims=True)) + lambda_init)
    acct = acct_ref[...]
    ot = acct[:HEAD_WIDTH, :] / acct[HEAD_WIDTH:HEAD_WIDTH + 1, :]
    ot = ot[:, :tq] - lam * ot[:, tq:]
    ms = jnp.mean(ot * ot, axis=0, keepdims=True)
    yt = ot * lax.rsqrt(ms + LN_EPS) * sub_ref[...] * (1.0 - lambda_init)
    o_ref[0] = yt.T.astype(o_ref.dtype)


def _alibi_features(slopes, t):
    pos = jnp.arange(t, dtype=jnp.int32)
    hi = (pos // ALIBI_SPLIT * ALIBI_SPLIT).astype(F32)[None, :]
    lo = (pos % ALIBI_SPLIT).astype(F32)[None, :]
    sl = slopes[:, None]
    one = jnp.ones((slopes.shape[0], t), F32)
    zero = jnp.zeros_like(one)
    pad = [zero] * (HEAD_WIDTH - 5)
    qf = jnp.stack([one, one, -sl * hi, -sl * lo, one] + pad, axis=-1)
    kf = jnp.stack([sl * hi, sl * lo, one, one, zero] + pad, axis=-1)
    assert ALIBI_OFFSET_LANE == 4
    return qf.astype(BF16), kf.astype(BF16)


def _diff_attention(proj3, slopes, lq1, lk1, lq2, lk2, subln, lambda_init, tq=512, tk=512):
    bsz, seq, _ = proj3.shape
    nh = N_ATTN_HEADS
    tq, tk = min(tq, seq), min(tk, seq)
    assert tq % tk == 0 and seq % tq == 0 and (2 * tq) % QUERY_BLOCK == 0
    qft = _alibi_features(slopes, tq)[0].transpose(0, 2, 1)
    kf = _alibi_features(slopes, tk)[1]
    di = jnp.arange(tq, dtype=jnp.int32)[None, None, :]
    dj = jnp.arange(tk, dtype=jnp.int32)[None, :, None]
    r = jnp.arange(tq // tk, dtype=jnp.int32)[:, None, None]
    bias_t = -slopes[:, None, None, None] * jnp.abs(di - dj - r * tk).astype(F32)[None]
    vec = lambda a: a.reshape(1, -1).astype(F32)
    small = lambda n: pl.BlockSpec((1, n), lambda b, h, i, s: (0, 0))
    grid_spec = pltpu.PrefetchScalarGridSpec(
        num_scalar_prefetch=1,
        grid=(bsz, nh, seq // tq),
        in_specs=[
            pl.BlockSpec((1, tq, HEAD_WIDTH), lambda b, h, i, s: (b, i, h)),
            pl.BlockSpec((1, seq, HEAD_WIDTH), lambda b, h, i, s: (b, 0, nh + h)),
            pl.BlockSpec((1, seq, HEAD_WIDTH), lambda b, h, i, s: (b, 0, 2 * nh + h)),
            pl.BlockSpec((1, HEAD_WIDTH, tq), lambda b, h, i, s: (h, 0, 0)),
            pl.BlockSpec((1, tk, HEAD_WIDTH), lambda b, h, i, s: (h, 0, 0)),
            pl.BlockSpec((1, tq // tk, tk, tq), lambda b, h, i, s: (h, 0, 0, 0)),
            small(DIFF_HEAD_DIM), small(DIFF_HEAD_DIM), small(DIFF_HEAD_DIM), small(DIFF_HEAD_DIM),
            pl.BlockSpec((HEAD_WIDTH, 1), lambda b, h, i, s: (0, 0)),
        ],
        out_specs=pl.BlockSpec((1, tq, HEAD_WIDTH), lambda b, h, i, s: (b, i, h)),
        scratch_shapes=[pltpu.VMEM((3, 2 * HEAD_WIDTH, 2 * tq), BF16),
                        pltpu.VMEM((seq // tk, HEAD_WIDTH + ONES_ROWS, tk), BF16),
                        pltpu.VMEM((tk, 2 * tq), F32),
                        pltpu.VMEM((tk, 2 * tq), BF16),
                        pltpu.VMEM((1, 2 * tq), F32),
                        pltpu.VMEM((1, 2 * tq), F32),
                        pltpu.VMEM((1, 2 * tq), F32),
                        pltpu.VMEM((HEAD_WIDTH + ONES_ROWS, 2 * tq), F32),
                        pltpu.VMEM((1, LANES), F32)],
    )
    return pl.pallas_call(
        functools.partial(_attn_kernel, tq=tq, tk=tk, seq=seq, lambda_init=lambda_init),
        grid_spec=grid_spec,
        out_shape=jax.ShapeDtypeStruct((bsz, seq, nh * HEAD_WIDTH), BF16),
        compiler_params=_cparams(("parallel", "parallel", "arbitrary")),
    )(slopes, proj3, proj3, proj3, qft, kf, bias_t,
      vec(lq1), vec(lk1), vec(lq2), vec(lk2), subln.reshape(-1, 1).astype(F32))


def _outproj_kernel(oa_ref, gb_ref, gc_ref, u_ref, cp_ref, up_ref, cn_ref, un_ref, cw_ref,
                    wa_ref, wc_ref, x_ref, g_ref, b_ref, x1_ref, x1b_ref, *, tm, seq, alpha):
    i = pl.program_id(0)
    cu = gc_ref[...].astype(F32) * u_ref[...].astype(F32)
    row = lax.broadcasted_iota(jnp.int32, cu.shape, 0)
    last = SUBLANES - 1
    prev_edge = cp_ref[last:last + 1, :].astype(F32) * up_ref[last:last + 1, :].astype(F32)
    next_edge = cn_ref[0:1, :].astype(F32) * un_ref[0:1, :].astype(F32)
    prev_edge = jnp.where((i * tm) % seq == 0, 0.0, prev_edge)
    next_edge = jnp.where(((i + 1) * tm) % seq == 0, 0.0, next_edge)
    cu_prev = jnp.where(row == 0, prev_edge, pltpu.roll(cu, 1, 0))
    cu_next = jnp.where(row == tm - 1, next_edge, pltpu.roll(cu, tm - 1, 0))
    cw = cw_ref[...]
    conv = cu_prev * cw[0:1, :] + cu * cw[1:2, :] + cu_next * cw[2:3, :]
    oc = (gb_ref[...].astype(F32) * conv).astype(BF16)
    y = _dot(oa_ref[...], wa_ref[...]) + _dot(oc, wc_ref[...])
    x1 = _layer_norm(alpha * x_ref[...] + y, g_ref[...], b_ref[...])
    x1_ref[...] = x1
    x1b_ref[...] = x1.astype(BF16)


def _out_projection(oattn, proj, x, conv_w, w_out, ln_g, ln_b, seq, alpha, tm=512):
    t, d = x.shape
    aw = oattn.shape[1]
    cwid = d - aw
    assert aw == cwid and proj.shape[1] == 3 * aw + 3 * cwid
    tm = min(tm, seq)
    hb = tm // SUBLANES
    nhb = t // SUBLANES
    cb = 3 * aw // cwid
    prev_map = lambda c: (lambda i: (jnp.maximum(i * hb - 1, 0), c))
    next_map = lambda c: (lambda i: (jnp.minimum((i + 1) * hb, nhb - 1), c))
    wa = w_out[:aw].astype(BF16)
    wc = w_out[aw:].astype(BF16)
    row = lambda a: a.reshape(1, -1).astype(F32)
    return pl.pallas_call(
        functools.partial(_outproj_kernel, tm=tm, seq=seq, alpha=alpha),
        grid=(t // tm,),
        in_specs=[
            pl.BlockSpec((tm, aw), lambda i: (i, 0)),
            pl.BlockSpec((tm, cwid), lambda i: (i, cb)),
            pl.BlockSpec((tm, cwid), lambda i: (i, cb + 1)),
            pl.BlockSpec((tm, cwid), lambda i: (i, cb + 2)),
            pl.BlockSpec((SUBLANES, cwid), prev_map(cb + 1)),
            pl.BlockSpec((SUBLANES, cwid), prev_map(cb + 2)),
            pl.BlockSpec((SUBLANES, cwid), next_map(cb + 1)),
            pl.BlockSpec((SUBLANES, cwid), next_map(cb + 2)),
            _resident((CONV_TAPS, cwid), lambda i: (0, 0)),
            _resident((aw, d), lambda i: (0, 0)),
            _resident((cwid, d), lambda i: (0, 0)),
            pl.BlockSpec((tm, d), lambda i: (i, 0)),
            _resident((1, d), lambda i: (0, 0)),
            _resident((1, d), lambda i: (0, 0)),
        ],
        out_specs=[pl.BlockSpec((tm, d), lambda i: (i, 0)),
                   pl.BlockSpec((tm, d), lambda i: (i, 0))],
        out_shape=[jax.ShapeDtypeStruct((t, d), F32), jax.ShapeDtypeStruct((t, d), BF16)],
        compiler_params=_cparams(("parallel",)),
    )(oattn, proj, proj, proj, proj, proj, proj, proj, conv_w.astype(F32), wa, wc, x,
      row(ln_g), row(ln_b))


def _memattn_kernel(x1b_ref, x1_ref, wq_ref, kt_ref, v_ref, wo_ref, g_ref, b_ref, rw_ref, rb_ref,
                    x2_ref, x2p_ref, te_ref, tg_ref, rk_ref, cnt_ref, carry_ref, *, tm, alpha):
    i = pl.program_id(0)
    d = x1_ref.shape[1]
    hd = d // MEM_HEADS

    @pl.when(i == 0)
    def _():
        carry_ref[...] = jnp.zeros(carry_ref.shape, F32)

    q = _dot(x1b_ref[...], wq_ref[...]).astype(BF16)
    scale = hd ** -0.5
    heads = []
    for hh in range(MEM_HEADS):
        cols = slice(hh * hd, (hh + 1) * hd)
        s = _dot(q[:, cols], kt_ref[0, cols, :]) * scale
        p = jnp.exp(s - jnp.max(s, axis=1, keepdims=True))
        p = p / jnp.sum(p, axis=1, keepdims=True)
        heads.append(_dot(p.astype(BF16), v_ref[:, cols]).astype(BF16))
    xa = _dot(jnp.concatenate(heads, axis=1), wo_ref[...])
    x2 = _layer_norm(alpha * x1_ref[...] + xa, g_ref[...], b_ref[...])
    x2_ref[...] = x2

    _store_row_tiles(x2p_ref, _pack_rows(x2[:, :d // 2], x2[:, d // 2:]))

    xh = x2.astype(BF16)
    xl = (x2 - xh.astype(F32)).astype(BF16)
    both = _dot(xh, rw_ref[...])
    logits = both[:, :LANES] + both[:, LANES:] + _dot(xl, rw_ref[:, :LANES]) + rb_ref[...]

    lane = lax.broadcasted_iota(jnp.int32, logits.shape, 1)
    work = logits
    vals, sels, idxs = [], [], []
    for _ in range(TOP_K):
        mk = jnp.max(work, axis=1, keepdims=True)
        idx = jnp.min(jnp.where(work == mk, lane, LANES), axis=1, keepdims=True)
        sel = lane == idx
        vals.append(mk)
        idxs.append(idx)
        sels.append(sel)
        work = jnp.where(sel, -jnp.inf, work)
    exps = [jnp.exp(v - vals[0]) for v in vals]
    denom = exps[0]
    for e in exps[1:]:
        denom = denom + e

    onehot = jnp.zeros(logits.shape, F32)
    for sel in sels:
        onehot = onehot + sel.astype(F32)
    lower = (lax.broadcasted_iota(jnp.int32, (tm, tm), 0)
             > lax.broadcasted_iota(jnp.int32, (tm, tm), 1)).astype(BF16)
    before = _dot(lower, onehot.astype(BF16)) + carry_ref[...]

    te = jnp.zeros(logits.shape, jnp.int32)
    tg = jnp.zeros(logits.shape, F32)
    rk = jnp.zeros(logits.shape, jnp.int32)
    for k in range(TOP_K):
        rank_k = jnp.sum(jnp.where(sels[k], before, 0.0), axis=1, keepdims=True)
        te = jnp.where(lane == k, idxs[k], te)
        tg = jnp.where(lane == k, exps[k] / denom, tg)
        rk = jnp.where(lane == k, rank_k.astype(jnp.int32), rk)
    te_ref[...] = te
    tg_ref[...] = tg
    rk_ref[...] = rk
    carry_ref[...] = carry_ref[...] + jnp.sum(onehot, axis=0, keepdims=True)
    cnt_ref[...] = carry_ref[...]


def _memory_attention(x1b, x1, kv, wq, wo, ln_g, ln_b, router_w, router_b, seq, n_mem, alpha, tm=256):
    t, d = x1.shape
    tm = min(tm, seq)
    ne = router_w.shape[1]
    assert ne <= LANES
    rw = jnp.zeros((d, LANES), F32).at[:, :ne].set(router_w.astype(F32))
    rwh = rw.astype(BF16)
    rwl = (rw - rwh.astype(F32)).astype(BF16)
    rb = jnp.full((1, LANES), NEG_BIG, F32).at[0, :ne].set(router_b.astype(F32))
    row = lambda a: a.reshape(1, -1).astype(F32)
    kt = kv[:, :d].reshape(t // seq, n_mem, d).transpose(0, 2, 1)
    tiles_per_seq = seq // tm
    tile_out = lambda w, dt: (pl.BlockSpec((tm, w), lambda i: (i, 0)), jax.ShapeDtypeStruct((t, w), dt))
    rh = d // 2 // LANES
    x2p_out = (pl.BlockSpec((tm * rh, LANES), lambda i: (i, 0)),
               jax.ShapeDtypeStruct((t * rh, LANES), jnp.uint32))
    outs = [tile_out(d, F32), x2p_out, tile_out(LANES, jnp.int32),
            tile_out(LANES, F32), tile_out(LANES, jnp.int32),
            (pl.BlockSpec((1, LANES), lambda i: (0, 0)), jax.ShapeDtypeStruct((1, LANES), F32))]
    return pl.pallas_call(
        functools.partial(_memattn_kernel, tm=tm, alpha=alpha),
        grid=(t // tm,),
        in_specs=[
            pl.BlockSpec((tm, d), lambda i: (i, 0)),
            pl.BlockSpec((tm, d), lambda i: (i, 0)),
            _resident((d, d), lambda i: (0, 0)),
            pl.BlockSpec((1, d, n_mem), lambda i: (i // tiles_per_seq, 0, 0)),
            pl.BlockSpec((n_mem, d), lambda i: (i // tiles_per_seq, 1)),
            _resident((d, d), lambda i: (0, 0)),
            _resident((1, d), lambda i: (0, 0)),
            _resident((1, d), lambda i: (0, 0)),
            _resident((d, 2 * LANES), lambda i: (0, 0)),
            _resident((1, LANES), lambda i: (0, 0)),
        ],
        out_specs=[o[0] for o in outs],
        out_shape=[o[1] for o in outs],
        scratch_shapes=[pltpu.VMEM((1, LANES), F32)],
        compiler_params=_cparams(("arbitrary",)),
    )(x1b, x1, wq.astype(BF16), kt, kv, wo.astype(BF16), row(ln_g), row(ln_b),
      jnp.concatenate([rwh, rwl], axis=1), rb)


def _dispatch_kernel(dest_ref, fill_ref, x2p_ref, xs_ref, zero_ref, stage_ref, sem, zsem,
                     *, rows, tm, n_fill, rh, n_steps):
    i = pl.program_id(0)

    @pl.when(i == 0)
    def _():
        zero_ref[...] = jnp.zeros(zero_ref.shape, zero_ref.dtype)

        def fill(n):
            start = pl.multiple_of(jnp.maximum(fill_ref[n], 0) * (tm * rh), tm * rh)
            return pltpu.make_async_copy(zero_ref, xs_ref.at[pl.ds(start, tm * rh)], zsem)

        for n in range(n_fill):
            @pl.when(fill_ref[n] >= 0)
            def _(n=n):
                fill(n).start()
        for n in range(n_fill):
            @pl.when(fill_ref[n] >= 0)
            def _(n=n):
                fill(n).wait()

    base = i * rows * TOP_K
    slot = i % 2
    stage_ref[slot] = x2p_ref[...]

    def issue(r, carry):
        src = pl.multiple_of(r * rh, rh)
        for k in range(TOP_K):
            dst = pl.multiple_of(dest_ref[base + r * TOP_K + k] * rh, rh)
            pltpu.make_async_copy(stage_ref.at[slot, pl.ds(src, rh)], xs_ref.at[pl.ds(dst, rh)],
                                  sem.at[slot]).start(priority=k % 2)
        return carry

    lax.fori_loop(0, rows, issue, 0, unroll=ISSUE_UNROLL)

    def drain(s):
        for _ in range(TOP_K):
            pltpu.make_async_copy(x2p_ref, xs_ref.at[pl.ds(0, rows * rh)], sem.at[s]).wait()

    @pl.when(i > 0)
    def _():
        drain(1 - slot)

    @pl.when(i == n_steps - 1)
    def _():
        drain(slot)


def _dispatch(dest, fill_tiles, x2p, n_rows, tm, rh, rows=512):
    t = x2p.shape[0] // rh
    rows = min(rows, t)
    n_fill = fill_tiles.shape[0]
    grid_spec = pltpu.PrefetchScalarGridSpec(
        num_scalar_prefetch=2,
        grid=(t // rows,),
        in_specs=[pl.BlockSpec((rows * rh, LANES), lambda i, d, z: (i, 0))],
        out_specs=pl.BlockSpec(memory_space=pl.ANY),
        scratch_shapes=[pltpu.VMEM((tm * rh, LANES), x2p.dtype),
                        pltpu.VMEM((2, rows * rh, LANES), x2p.dtype),
                        pltpu.SemaphoreType.DMA((2,)), pltpu.SemaphoreType.DMA],
    )
    return pl.pallas_call(
        functools.partial(_dispatch_kernel, rows=rows, tm=tm, n_fill=n_fill, rh=rh,
                          n_steps=t // rows),
        grid_spec=grid_spec,
        out_shape=jax.ShapeDtypeStruct((n_rows * rh, LANES), x2p.dtype),
        compiler_params=_cparams(("arbitrary",)),
    )(dest, fill_tiles, x2p)


def _pack_rows(lo, hi):
    lo = lax.bitcast_convert_type(lo.astype(BF16).astype(F32), jnp.uint32)
    hi = lax.bitcast_convert_type(hi.astype(BF16).astype(F32), jnp.uint32)
    return (lo >> 16) | (hi & jnp.uint32(0xFFFF0000))


def _store_row_tiles(ref, words):
    rows, width = words.shape
    chunks = width // LANES
    for c in range(chunks):
        ref[pl.ds(c, rows, stride=chunks), :] = words[:, c * LANES:(c + 1) * LANES]


def _load_row_tiles(ref, chunks):
    rows = ref.shape[0] // chunks
    return jnp.concatenate([ref[pl.ds(c, rows, stride=chunks), :] for c in range(chunks)], axis=1)


def _unpack_rows(words):
    lo = lax.bitcast_convert_type(words << 16, F32)
    hi = lax.bitcast_convert_type(words & jnp.uint32(0xFFFF0000), F32)
    return lo, hi


def _expert_up_kernel(ie_ref, in_ref, irt_ref, ivalid_ref, ifirst_ref, xs_ref, wg_ref, wu_ref,
                      bg_ref, bu_ref, h_ref, wgb_ref, wub_ref):
    w = pl.program_id(0)

    @pl.when(ifirst_ref[w] == 1)
    def _():
        wgb_ref[...] = wg_ref[0].astype(BF16)
        wub_ref[...] = wu_ref[0].astype(BF16)

    @pl.when(ivalid_ref[w] == 1)
    def _():
        rh = wgb_ref.shape[0] // 2 // LANES
        lo, hi = (a.astype(BF16) for a in _unpack_rows(_load_row_tiles(xs_ref, rh)))
        half = lo.shape[1]
        gate = _dot(lo, wgb_ref[:half, :]) + _dot(hi, wgb_ref[half:, :]) + bg_ref[0]
        up = _dot(lo, wub_ref[:half, :]) + _dot(hi, wub_ref[half:, :]) + bu_ref[0]
        gate = jnp.minimum(gate, SWIGLU_LIMIT)
        up = jnp.clip(up, -SWIGLU_LIMIT, SWIGLU_LIMIT)
        act = gate * jax.nn.sigmoid(SWIGLU_ALPHA * gate) * (up + 1.0)
        h_ref[...] = act.astype(h_ref.dtype)

    @pl.when(ivalid_ref[w] == 0)
    def _():
        h_ref[...] = jnp.zeros(h_ref.shape, h_ref.dtype)


def _expert_down_kernel(ie_ref, in_ref, irt_ref, ivalid_ref, ifirst_ref, h_ref, wd_ref, bd_ref,
                        o_ref, wdb_ref):
    w = pl.program_id(0)

    @pl.when(ifirst_ref[w] == 1)
    def _():
        wdb_ref[...] = wd_ref[0].astype(BF16)

    @pl.when(ivalid_ref[w] == 1)
    def _():
        out = _dot(h_ref[...], wdb_ref[...]) + bd_ref[0]
        half = out.shape[1] // 2
        _store_row_tiles(o_ref, _pack_rows(out[:, :half], out[:, half:]))

    @pl.when(ivalid_ref[w] == 0)
    def _():
        o_ref[...] = jnp.zeros(o_ref.shape, o_ref.dtype)


def _work_items(tiles_e, n_tiles_max, n_col):
    ne = tiles_e.shape[0]
    cum = jnp.cumsum(tiles_e)
    total_tiles = cum[-1]
    first_tile = cum - tiles_e
    item_end = cum * n_col
    w = jnp.arange(n_tiles_max * n_col, dtype=jnp.int32)
    e = jnp.minimum(jnp.sum(w[:, None] >= item_end[None, :], axis=1), ne - 1).astype(jnp.int32)
    te = jnp.maximum(tiles_e[e], 1)
    local = w - (item_end[e] - tiles_e[e] * n_col)
    valid = w < total_tiles * n_col
    spare = jnp.maximum(n_tiles_max - total_tiles, 1)
    j = w - total_tiles * n_col
    col = jnp.where(valid, local // te, j // spare)
    rt = jnp.where(valid, first_tile[e] + local % te, total_tiles + j % spare)
    first = jnp.where(valid, (local % te) == 0, False)
    i32 = lambda a: a.astype(jnp.int32)
    return i32(e), i32(col), i32(rt), i32(valid), i32(first)


def _expert_up(items, xs, w_gate, w_up, b_gate, b_up, tm, tn):
    ne, d, f = w_gate.shape
    rh = d // 2 // LANES
    n_rows = xs.shape[0] // rh
    tn = min(tn, f)
    n_items = items[0].shape[0]
    wspec = pl.BlockSpec((1, d, tn), lambda w, ie, ic, irt, iv, ifi: (ie[w], 0, ic[w]))
    bspec = pl.BlockSpec((1, 1, tn), lambda w, ie, ic, irt, iv, ifi: (ie[w], 0, ic[w]))
    grid_spec = pltpu.PrefetchScalarGridSpec(
        num_scalar_prefetch=5,
        grid=(n_items,),
        in_specs=[pl.BlockSpec((tm * rh, LANES), lambda w, ie, ic, irt, iv, ifi: (irt[w], 0)),
                  wspec, wspec, bspec, bspec],
        out_specs=pl.BlockSpec((tm, tn), lambda w, ie, ic, irt, iv, ifi: (irt[w], ic[w])),
        scratch_shapes=[pltpu.VMEM((d, tn), BF16), pltpu.VMEM((d, tn), BF16)],
    )
    return pl.pallas_call(
        _expert_up_kernel,
        grid_spec=grid_spec,
        out_shape=jax.ShapeDtypeStruct((n_rows, f), BF16),
        compiler_params=_cparams(("arbitrary",)),
    )(*items, xs, w_gate, w_up, b_gate.reshape(ne, 1, f), b_up.reshape(ne, 1, f))


def _expert_down(items, h, w_down, b_down, tm):
    n_rows, f = h.shape
    ne, _, d = w_down.shape
    rh = d // 2 // LANES
    n_items = items[0].shape[0]
    grid_spec = pltpu.PrefetchScalarGridSpec(
        num_scalar_prefetch=5,
        grid=(n_items,),
        in_specs=[pl.BlockSpec((tm, f), lambda w, ie, ic, irt, iv, ifi: (irt[w], 0)),
                  pl.BlockSpec((1, f, d), lambda w, ie, ic, irt, iv, ifi: (ie[w], 0, 0)),
                  pl.BlockSpec((1, 1, d), lambda w, ie, ic, irt, iv, ifi: (ie[w], 0, 0))],
        out_specs=pl.BlockSpec((tm * rh, LANES), lambda w, ie, ic, irt, iv, ifi: (irt[w], 0)),
        scratch_shapes=[pltpu.VMEM((f, d), BF16)],
    )
    return pl.pallas_call(
        _expert_down_kernel,
        grid_spec=grid_spec,
        out_shape=jax.ShapeDtypeStruct((n_rows * rh, LANES), jnp.uint32),
        compiler_params=_cparams(("arbitrary",)),
    )(*items, h, w_down, b_down.reshape(ne, 1, d))


def _combine_kernel(dest_ref, os_ref, tg_ref, x2_ref, g_ref, b_ref, o_ref, buf_ref, sem,
                    *, rows, alpha, rh, n_steps):
    i = pl.program_id(0)
    slot = i % 2

    def gather(step, into):
        base = step * rows * TOP_K

        def issue(r, carry):
            dst = pl.multiple_of(r * rh, rh)
            for k in range(TOP_K):
                src = pl.multiple_of(dest_ref[base + r * TOP_K + k] * rh, rh)
                pltpu.make_async_copy(os_ref.at[pl.ds(src, rh)], buf_ref.at[into, k, pl.ds(dst, rh)],
                                      sem.at[into]).start(priority=k % 2)
            return carry

        lax.fori_loop(0, rows, issue, 0, unroll=ISSUE_UNROLL)

    @pl.when(i == 0)
    def _():
        gather(0, 0)

    @pl.when(i + 1 < n_steps)
    def _():
        gather(i + 1, 1 - slot)

    for k in range(TOP_K):
        pltpu.make_async_copy(os_ref.at[pl.ds(0, rows * rh)], buf_ref.at[slot, k],
                              sem.at[slot]).wait()

    tg = tg_ref[...]
    lo_sum = hi_sum = None
    for k in range(TOP_K):
        lo, hi = _unpack_rows(_load_row_tiles(buf_ref.at[slot, k], rh))
        gate = tg[:, k:k + 1]
        lo_sum = gate * lo if k == 0 else lo_sum + gate * lo
        hi_sum = gate * hi if k == 0 else hi_sum + gate * hi
    y = jnp.concatenate([lo_sum, hi_sum], axis=1)
    o_ref[...] = _layer_norm(alpha * x2_ref[...] + y, g_ref[...], b_ref[...])


def _combine(dest, out_sorted, tg, x2, ln_g, ln_b, alpha, rows=256):
    t, d = x2.shape
    rows = min(rows, t)
    rh = d // 2 // LANES
    row = lambda a: a.reshape(1, -1).astype(F32)
    grid_spec = pltpu.PrefetchScalarGridSpec(
        num_scalar_prefetch=1,
        grid=(t // rows,),
        in_specs=[pl.BlockSpec(memory_space=pl.ANY),
                  pl.BlockSpec((rows, LANES), lambda i, s: (i, 0)),
                  pl.BlockSpec((rows, d), lambda i, s: (i, 0)),
                  pl.BlockSpec((1, d), lambda i, s: (0, 0)),
                  pl.BlockSpec((1, d), lambda i, s: (0, 0))],
        out_specs=pl.BlockSpec((rows, d), lambda i, s: (i, 0)),
        scratch_shapes=[pltpu.VMEM((2, TOP_K, rows * rh, LANES), jnp.uint32),
                        pltpu.SemaphoreType.DMA((2,))],
    )
    return pl.pallas_call(
        functools.partial(_combine_kernel, rows=rows, alpha=alpha, rh=rh, n_steps=t // rows),
        grid_spec=grid_spec,
        out_shape=jax.ShapeDtypeStruct((t, d), F32),
        compiler_params=_cparams(("arbitrary",)),
    )(dest, out_sorted, tg, x2, row(ln_g), row(ln_b))


def _moe(x2, x2p, te, tg, rk, counts, w_gate, b_gate, w_up, b_up, w_down, b_down, ln_g, ln_b,
         alpha, tm=512, tn_up=1024):
    t, d = x2.shape
    ne = w_gate.shape[0]
    n_assign = t * TOP_K
    tm = min(tm, t)
    n_tiles_max = -(-n_assign // tm) + ne
    n_rows = n_tiles_max * tm

    counts = counts[0, :ne].astype(jnp.int32)
    tiles_e = (counts + tm - 1) // tm
    pad_start = (jnp.cumsum(tiles_e) - tiles_e) * tm
    dest = (pad_start[te[:, :TOP_K]] + rk[:, :TOP_K]).reshape(n_assign).astype(jnp.int32)

    cum_tiles = jnp.cumsum(tiles_e)
    last_tile = jnp.where(tiles_e > 0, cum_tiles - 1, -1)
    tail = cum_tiles[-1] + jnp.arange(ne, dtype=jnp.int32)
    fill_tiles = jnp.concatenate([last_tile, jnp.where(tail < n_tiles_max, tail, -1)])

    xs = _dispatch(dest, fill_tiles.astype(jnp.int32), x2p, n_rows, tm, d // 2 // LANES)
    f = w_gate.shape[2]
    tn_up = min(tn_up, f)
    items_up = _work_items(tiles_e, n_tiles_max, f // tn_up)
    h = _expert_up(items_up, xs, w_gate, w_up, b_gate, b_up, tm, tn_up)
    items_dn = _work_items(tiles_e, n_tiles_max, 1)
    out_sorted = _expert_down(items_dn, h, w_down, b_down, tm)
    return _combine(dest, out_sorted, tg, x2, ln_g, ln_b, alpha)


def kernel(x, mem, w_in, conv_w, attn_subln_w, lambda_q1, lambda_k1, lambda_q2, lambda_k2, w_out, ln1_g, ln1_b, mem_wq, mem_wkv, mem_wo, ln2_g, ln2_b, router_w, router_b, w_gate, b_gate, w_up, b_up, w_down, b_down, ln3_g, ln3_b):
    bsz, seq, d = x.shape
    n_mem = mem.shape[1]
    depth = w_in.shape[0]
    alpha = (2.0 * depth) ** 0.25
    slopes = jnp.exp2(-8.0 / N_ATTN_HEADS * jnp.arange(1, N_ATTN_HEADS + 1, dtype=F32))
    xf = x.reshape(bsz * seq, d)
    memf = mem.reshape(bsz * n_mem, d)
    for l in range(depth):
        lambda_init = 0.8 - 0.6 * math.exp(-0.3 * l)
        proj = _projection(xf, w_in[l].astype(BF16), 1024, 1024)
        oattn = _diff_attention(proj.reshape(bsz, seq, -1), slopes, lambda_q1[l], lambda_k1[l],
                                lambda_q2[l], lambda_k2[l], attn_subln_w[l], lambda_init)
        x1, x1b = _out_projection(oattn.reshape(bsz * seq, -1), proj, xf, conv_w[l], w_out[l],
                                  ln1_g[l], ln1_b[l], seq, alpha)
        kv = _projection(memf, mem_wkv[l].astype(BF16), 1024, 1024)
        x2, x2p, te, tg, rk, counts = _memory_attention(
            x1b, x1, kv, mem_wq[l], mem_wo[l], ln2_g[l], ln2_b[l], router_w[l], router_b[l],
            seq, n_mem, alpha)
        xf = _moe(x2, x2p, te, tg, rk, counts, w_gate[l], b_gate[l], w_up[l], b_up[l],
                  w_down[l], b_down[l], ln3_g[l], ln3_b[l], alpha)
    return xf.reshape(bsz, seq, d)
```

```python
import functools
import math

import jax
import jax.numpy as jnp
from jax import lax
from jax.experimental import pallas as pl
from jax.experimental.pallas import tpu as pltpu

N_ATTN_HEADS = 8
DIFF_HEAD_DIM = 64
HEAD_WIDTH = 2 * DIFF_HEAD_DIM
MEM_HEADS = 4
TOP_K = 4
CONV_TAPS = 3
SWIGLU_LIMIT = 7.0
SWIGLU_ALPHA = 1.702
LN_EPS = 1e-5

LANES = 128
SUBLANES = 8
VMEM_LIMIT = 56 * 1024 * 1024
NEG_BIG = -1e30
ALIBI_SPLIT = 16
ALIBI_OFFSET_LANE = 4
QUERY_BLOCK = 256
ONES_ROWS = 16
UNDERFLOW_NATS = 104.0
NORM_SLACK = 1.01
ISSUE_UNROLL = 8

F32 = jnp.float32
BF16 = jnp.bfloat16


def _cparams(sem, vmem=VMEM_LIMIT):
    return pltpu.CompilerParams(dimension_semantics=sem, vmem_limit_bytes=vmem)


def _resident(shape, index_map):
    return pl.BlockSpec(shape, index_map, pipeline_mode=pl.Buffered(1))


def _layer_norm(z, g, b):
    mu = jnp.mean(z, axis=-1, keepdims=True)
    d = z - mu
    var = jnp.mean(d * d, axis=-1, keepdims=True)
    return d * lax.rsqrt(var + LN_EPS) * g + b


def _dot(a, b):
    return jnp.dot(a, b, preferred_element_type=F32)


def _proj_kernel(x_ref, w_ref, o_ref, xb_ref):
    @pl.when(pl.program_id(1) == 0)
    def _():
        xb_ref[...] = x_ref[...].astype(BF16)

    o_ref[...] = _dot(xb_ref[...], w_ref[...]).astype(o_ref.dtype)


def _projection(x, w, tm, tn):
    m, k = x.shape
    n = w.shape[1]
    tm, tn = min(tm, m), min(tn, n)
    return pl.pallas_call(
        _proj_kernel,
        grid=(m // tm, n // tn),
        in_specs=[pl.BlockSpec((tm, k), lambda i, j: (i, 0)),
                  pl.BlockSpec((k, tn), lambda i, j: (0, j))],
        out_specs=pl.BlockSpec((tm, tn), lambda i, j: (i, j)),
        out_shape=jax.ShapeDtypeStruct((m, n), BF16),
        scratch_shapes=[pltpu.VMEM((tm, k), BF16)],
        compiler_params=_cparams(("parallel", "arbitrary")),
    )(x, w)


def _attn_kernel(slopes_ref, q_ref, k_ref, v_ref, qft_ref, kf_ref, bdt_ref, lq1_ref, lk1_ref, lq2_ref,
                 lk2_ref, sub_ref, o_ref, lhst_ref, vt_ref, st_ref, pt_ref, m_ref, mx_ref, alpha_ref,
                 acct_ref, kn_ref, *, tq, tk, seq, lambda_init):
    h = pl.program_id(1)
    qi = pl.program_id(2)
    slope = slopes_ref[h]

    @pl.when(qi == 0)
    def _():
        half_lane = lax.broadcasted_iota(jnp.int32, (tk, HEAD_WIDTH), 1) < DIFF_HEAD_DIM
        kn1 = jnp.zeros((1, 1), F32)
        kn2 = jnp.zeros((1, 1), F32)
        for c in range(seq // tk):
            vt_ref[c, :HEAD_WIDTH, :] = v_ref[0, c * tk:(c + 1) * tk, :].astype(F32).T.astype(BF16)
            vt_ref[c, HEAD_WIDTH:, :] = jnp.ones((ONES_ROWS, tk), BF16)
            kc = k_ref[0, c * tk:(c + 1) * tk, :].astype(F32)
            sq = kc * kc
            n1 = jnp.sum(jnp.where(half_lane, sq, 0.0), axis=1, keepdims=True)
            n2 = jnp.sum(jnp.where(half_lane, 0.0, sq), axis=1, keepdims=True)
            kn1 = jnp.maximum(kn1, jnp.max(n1, axis=0, keepdims=True))
            kn2 = jnp.maximum(kn2, jnp.max(n2, axis=0, keepdims=True))
        kn_ref[...] = jnp.where(lax.broadcasted_iota(jnp.int32, kn_ref.shape, 1) == 0, kn1, kn2)

    qt = (q_ref[0].astype(F32) * (DIFF_HEAD_DIM ** -0.5)).T
    row = lax.broadcasted_iota(jnp.int32, qt.shape, 0)

    qsq = qt * qt
    qn1 = jnp.max(jnp.sum(jnp.where(row < DIFF_HEAD_DIM, qsq, 0.0), axis=0, keepdims=True),
                  axis=1, keepdims=True)
    qn2 = jnp.max(jnp.sum(jnp.where(row >= DIFF_HEAD_DIM, qsq, 0.0), axis=0, keepdims=True),
                  axis=1, keepdims=True)
    kn = kn_ref[...]
    qk = jnp.sqrt(jnp.maximum(qn1 * kn[:, 0:1], qn2 * kn[:, 1:2]))
    reach = jnp.minimum((UNDERFLOW_NATS + 2.0 * NORM_SLACK * qk) / slope, 4.0 * seq)
    q_lo = (qi * tq).astype(F32)
    lo_f = jnp.floor((q_lo - (tk - 1) - reach) / tk) + 1.0
    hi_f = jnp.ceil((q_lo + (tq - 1) + reach) / tk)
    n_tiles = seq // tk
    per_q = tq // tk
    lo = jnp.clip(lo_f.astype(jnp.int32)[0, 0], 0, qi * per_q)
    hi = jnp.clip(hi_f.astype(jnp.int32)[0, 0], (qi + 1) * per_q, n_tiles)
    if n_tiles > 1:
        short = hi - lo < 2
        lo, hi = (jnp.where(short & (lo > 0), lo - 1, lo), jnp.where(short & (lo == 0), hi + 1, hi))
    qqt = jnp.concatenate([jnp.where(row < DIFF_HEAD_DIM, qt, 0.0),
                           jnp.where(row >= DIFF_HEAD_DIM, qt, 0.0)], axis=1).astype(BF16)
    fft = jnp.concatenate([qft_ref[0], qft_ref[0]], axis=1)
    lhst_ref[0] = jnp.concatenate([qqt, fft], axis=0)
    lhst_ref[1] = jnp.concatenate([qqt, -fft], axis=0)
    lhst_ref[2] = jnp.concatenate([qqt, jnp.zeros_like(fft)], axis=0)

    m_ref[...] = jnp.full(m_ref.shape, NEG_BIG, F32)
    acct_ref[...] = jnp.zeros(acct_ref.shape, F32)
    kf = kf_ref[0]
    klane = lax.broadcasted_iota(jnp.int32, (tk, HEAD_WIDTH), 1)

    blocks = [slice(qb * QUERY_BLOCK, (qb + 1) * QUERY_BLOCK)
              for qb in range(2 * tq // QUERY_BLOCK)]

    def scores(kt):
        start = pl.multiple_of(kt * tk, tk)
        overlap = kt - qi * per_q
        variant = jnp.where(overlap < 0, 0, jnp.where(overlap >= per_q, 1, 2))
        c = slope * (kt * tk - qi * tq).astype(F32)
        feat = jnp.where(klane == ALIBI_OFFSET_LANE, jnp.full(kf.shape, c, F32).astype(BF16), kf)
        k_aug = jnp.concatenate([k_ref[0, pl.ds(start, tk), :], feat], axis=1)
        for cols in blocks:
            st = _dot(k_aug, lhst_ref[variant, :, cols])
            st_ref[:, cols] = st
            mx_ref[:, cols] = jnp.max(st, axis=0, keepdims=True)

        @pl.when(variant == 2)
        def _():
            bias_t = bdt_ref[0, jnp.clip(overlap, 0, per_q - 1)]
            for cols in blocks:
                di = cols.start % tq
                st = st_ref[:, cols] + bias_t[:, di:di + QUERY_BLOCK]
                st_ref[:, cols] = st
                mx_ref[:, cols] = jnp.max(st, axis=0, keepdims=True)

    def probs():
        for cols in blocks:
            st = st_ref[:, cols]
            m_prev = m_ref[:, cols]
            m_new = jnp.maximum(m_prev, mx_ref[:, cols])
            alpha_ref[:, cols] = jnp.exp(m_prev - m_new)
            pt_ref[:, cols] = jnp.exp(st - m_new).astype(BF16)
            m_ref[:, cols] = m_new

    def accumulate(kt):
        vt = vt_ref[kt]
        for cols in blocks:
            acct_ref[:, cols] = alpha_ref[:, cols] * acct_ref[:, cols] + _dot(vt, pt_ref[:, cols])

    def step(j, carry):
        accumulate(j - 2)
        probs()
        scores(j)
        return carry

    scores(lo)
    if n_tiles > 1:
        probs()
        scores(lo + 1)
        lax.fori_loop(lo + 2, hi, step, 0)
        accumulate(hi - 2)
    probs()
    accumulate(hi - 1)

    lam = (jnp.exp(jnp.sum(lq1_ref[...] * lk1_ref[...], keepdims=True))
           - jnp.exp(jnp.sum(lq2_ref[...] * lk2_ref[...], keepdims=True)) + lambda_init)
    acct = acct_ref[...]
    ot = acct[:HEAD_WIDTH, :] / acct[HEAD_WIDTH:HEAD_WIDTH + 1, :]
    ot = ot[:, :tq] - lam * ot[:, tq:]
    ms = jnp.mean(ot * ot, axis=0, keepdims=True)
    yt = ot * lax.rsqrt(ms + LN_EPS) * sub_ref[...] * (1.0 - lambda_init)
    o_ref[0] = yt.T.astype(o_ref.dtype)


def _alibi_features(slopes, t):
    pos = jnp.arange(t, dtype=jnp.int32)
    hi = (pos // ALIBI_SPLIT * ALIBI_SPLIT).astype(F32)[None, :]
    lo = (pos % ALIBI_SPLIT).astype(F32)[None, :]
    sl = slopes[:, None]
    one = jnp.ones((slopes.shape[0], t), F32)
    zero = jnp.zeros_like(one)
    pad = [zero] * (HEAD_WIDTH - 5)
    qf = jnp.stack([one, one, -sl * hi, -sl * lo, one] + pad, axis=-1)
    kf = jnp.stack([sl * hi, sl * lo, one, one, zero] + pad, axis=-1)
    assert ALIBI_OFFSET_LANE == 4
    return qf.astype(BF16), kf.astype(BF16)


def _diff_attention(proj3, slopes, lq1, lk1, lq2, lk2, subln, lambda_init, tq=512, tk=512):
    bsz, seq, _ = proj3.shape
    nh = N_ATTN_HEADS
    tq, tk = min(tq, seq), min(tk, seq)
    assert tq % tk == 0 and seq % tq == 0 and (2 * tq) % QUERY_BLOCK == 0
    qft = _alibi_features(slopes, tq)[0].transpose(0, 2, 1)
    kf = _alibi_features(slopes, tk)[1]
    di = jnp.arange(tq, dtype=jnp.int32)[None, None, :]
    dj = jnp.arange(tk, dtype=jnp.int32)[None, :, None]
    r = jnp.arange(tq // tk, dtype=jnp.int32)[:, None, None]
    bias_t = -slopes[:, None, None, None] * jnp.abs(di - dj - r * tk).astype(F32)[None]
    vec = lambda a: a.reshape(1, -1).astype(F32)
    small = lambda n: pl.BlockSpec((1, n), lambda b, h, i, s: (0, 0))
    grid_spec = pltpu.PrefetchScalarGridSpec(
        num_scalar_prefetch=1,
        grid=(bsz, nh, seq // tq),
        in_specs=[
            pl.BlockSpec((1, tq, HEAD_WIDTH), lambda b, h, i, s: (b, i, h)),
            pl.BlockSpec((1, seq, HEAD_WIDTH), lambda b, h, i, s: (b, 0, nh + h)),
            pl.BlockSpec((1, seq, HEAD_WIDTH), lambda b, h, i, s: (b, 0, 2 * nh + h)),
            pl.BlockSpec((1, HEAD_WIDTH, tq), lambda b, h, i, s: (h, 0, 0)),
            pl.BlockSpec((1, tk, HEAD_WIDTH), lambda b, h, i, s: (h, 0, 0)),
            pl.BlockSpec((1, tq // tk, tk, tq), lambda b, h, i, s: (h, 0, 0, 0)),
            small(DIFF_HEAD_DIM), small(DIFF_HEAD_DIM), small(DIFF_HEAD_DIM), small(DIFF_HEAD_DIM),
            pl.BlockSpec((HEAD_WIDTH, 1), lambda b, h, i, s: (0, 0)),
        ],
        out_specs=pl.BlockSpec((1, tq, HEAD_WIDTH), lambda b, h, i, s: (b, i, h)),
        scratch_shapes=[pltpu.VMEM((3, 2 * HEAD_WIDTH, 2 * tq), BF16),
                        pltpu.VMEM((seq // tk, HEAD_WIDTH + ONES_ROWS, tk), BF16),
                        pltpu.VMEM((tk, 2 * tq), F32),
                        pltpu.VMEM((tk, 2 * tq), BF16),
                        pltpu.VMEM((1, 2 * tq), F32),
                        pltpu.VMEM((1, 2 * tq), F32),
                        pltpu.VMEM((1, 2 * tq), F32),
                        pltpu.VMEM((HEAD_WIDTH + ONES_ROWS, 2 * tq), F32),
                        pltpu.VMEM((1, LANES), F32)],
    )
    return pl.pallas_call(
        functools.partial(_attn_kernel, tq=tq, tk=tk, seq=seq, lambda_init=lambda_init),
        grid_spec=grid_spec,
        out_shape=jax.ShapeDtypeStruct((bsz, seq, nh * HEAD_WIDTH), BF16),
        compiler_params=_cparams(("parallel", "parallel", "arbitrary")),
    )(slopes, proj3, proj3, proj3, qft, kf, bias_t,
      vec(lq1), vec(lk1), vec(lq2), vec(lk2), subln.reshape(-1, 1).astype(F32))


def _outproj_kernel(oa_ref, gb_ref, gc_ref, u_ref, cp_ref, up_ref, cn_ref, un_ref, cw_ref,
                    wa_ref, wc_ref, x_ref, g_ref, b_ref, x1_ref, x1b_ref, *, tm, seq, alpha):
    i = pl.program_id(0)
    cu = gc_ref[...].astype(F32) * u_ref[...].astype(F32)
    row = lax.broadcasted_iota(jnp.int32, cu.shape, 0)
    last = SUBLANES - 1
    prev_edge = cp_ref[last:last + 1, :].astype(F32) * up_ref[last:last + 1, :].astype(F32)
    next_edge = cn_ref[0:1, :].astype(F32) * un_ref[0:1, :].astype(F32)
    prev_edge = jnp.where((i * tm) % seq == 0, 0.0, prev_edge)
    next_edge = jnp.where(((i + 1) * tm) % seq == 0, 0.0, next_edge)
    cu_prev = jnp.where(row == 0, prev_edge, pltpu.roll(cu, 1, 0))
    cu_next = jnp.where(row == tm - 1, next_edge, pltpu.roll(cu, tm - 1, 0))
    cw = cw_ref[...]
    conv = cu_prev * cw[0:1, :] + cu * cw[1:2, :] + cu_next * cw[2:3, :]
    oc = (gb_ref[...].astype(F32) * conv).astype(BF16)
    y = _dot(oa_ref[...], wa_ref[...]) + _dot(oc, wc_ref[...])
    x1 = _layer_norm(alpha * x_ref[...] + y, g_ref[...], b_ref[...])
    x1_ref[...] = x1
    x1b_ref[...] = x1.astype(BF16)


def _out_projection(oattn, proj, x, conv_w, w_out, ln_g, ln_b, seq, alpha, tm=512):
    t, d = x.shape
    aw = oattn.shape[1]
    cwid = d - aw
    assert aw == cwid and proj.shape[1] == 3 * aw + 3 * cwid
    tm = min(tm, seq)
    hb = tm // SUBLANES
    nhb = t // SUBLANES
    cb = 3 * aw // cwid
    prev_map = lambda c: (lambda i: (jnp.maximum(i * hb - 1, 0), c))
    next_map = lambda c: (lambda i: (jnp.minimum((i + 1) * hb, nhb - 1), c))
    wa = w_out[:aw].astype(BF16)
    wc = w_out[aw:].astype(BF16)
    row = lambda a: a.reshape(1, -1).astype(F32)
    return pl.pallas_call(
        functools.partial(_outproj_kernel, tm=tm, seq=seq, alpha=alpha),
        grid=(t // tm,),
        in_specs=[
            pl.BlockSpec((tm, aw), lambda i: (i, 0)),
            pl.BlockSpec((tm, cwid), lambda i: (i, cb)),
            pl.BlockSpec((tm, cwid), lambda i: (i, cb + 1)),
            pl.BlockSpec((tm, cwid), lambda i: (i, cb + 2)),
            pl.BlockSpec((SUBLANES, cwid), prev_map(cb + 1)),
            pl.BlockSpec((SUBLANES, cwid), prev_map(cb + 2)),
            pl.BlockSpec((SUBLANES, cwid), next_map(cb + 1)),
            pl.BlockSpec((SUBLANES, cwid), next_map(cb + 2)),
            _resident((CONV_TAPS, cwid), lambda i: (0, 0)),
            _resident((aw, d), lambda i: (0, 0)),
            _resident((cwid, d), lambda i: (0, 0)),
            pl.BlockSpec((tm, d), lambda i: (i, 0)),
            _resident((1, d), lambda i: (0, 0)),
            _resident((1, d), lambda i: (0, 0)),
        ],
        out_specs=[pl.BlockSpec((tm, d), lambda i: (i, 0)),
                   pl.BlockSpec((tm, d), lambda i: (i, 0))],
        out_shape=[jax.ShapeDtypeStruct((t, d), F32), jax.ShapeDtypeStruct((t, d), BF16)],
        compiler_params=_cparams(("parallel",)),
    )(oattn, proj, proj, proj, proj, proj, proj, proj, conv_w.astype(F32), wa, wc, x,
      row(ln_g), row(ln_b))


def _memattn_kernel(x1b_ref, x1_ref, wq_ref, kt_ref, v_ref, wo_ref, g_ref, b_ref, rw_ref, rb_ref,
                    x2_ref, x2p_ref, te_ref, tg_ref, rk_ref, cnt_ref, carry_ref, *, tm, alpha):
    i = pl.program_id(0)
    d = x1_ref.shape[1]
    hd = d // MEM_HEADS

    @pl.when(i == 0)
    def _():
        carry_ref[...] = jnp.zeros(carry_ref.shape, F32)

    q = _dot(x1b_ref[...], wq_ref[...]).astype(BF16)
    scale = hd ** -0.5
    heads = []
    for hh in range(MEM_HEADS):
        cols = slice(hh * hd, (hh + 1) * hd)
        s = _dot(q[:, cols], kt_ref[0, cols, :]) * scale
        p = jnp.exp(s - jnp.max(s, axis=1, keepdims=True))
        p = p / jnp.sum(p, axis=1, keepdims=True)
        heads.append(_dot(p.astype(BF16), v_ref[:, cols]).astype(BF16))
    xa = _dot(jnp.concatenate(heads, axis=1), wo_ref[...])
    x2 = _layer_norm(alpha * x1_ref[...] + xa, g_ref[...], b_ref[...])
    x2_ref[...] = x2

    _store_row_tiles(x2p_ref, _pack_rows(x2[:, :d // 2], x2[:, d // 2:]))

    xh = x2.astype(BF16)
    xl = (x2 - xh.astype(F32)).astype(BF16)
    both = _dot(xh, rw_ref[...])
    logits = both[:, :LANES] + both[:, LANES:] + _dot(xl, rw_ref[:, :LANES]) + rb_ref[...]

    lane = lax.broadcasted_iota(jnp.int32, logits.shape, 1)
    work = logits
    vals, sels, idxs = [], [], []
    for _ in range(TOP_K):
        mk = jnp.max(work, axis=1, keepdims=True)
        idx = jnp.min(jnp.where(work == mk, lane, LANES), axis=1, keepdims=True)
        sel = lane == idx
        vals.append(mk)
        idxs.append(idx)
        sels.append(sel)
        work = jnp.where(sel, -jnp.inf, work)
    exps = [jnp.exp(v - vals[0]) for v in vals]
    denom = exps[0]
    for e in exps[1:]:
        denom = denom + e

    onehot = jnp.zeros(logits.shape, F32)
    for sel in sels:
        onehot = onehot + sel.astype(F32)
    lower = (lax.broadcasted_iota(jnp.int32, (tm, tm), 0)
             > lax.broadcasted_iota(jnp.int32, (tm, tm), 1)).astype(BF16)
    before = _dot(lower, onehot.astype(BF16)) + carry_ref[...]

    te = jnp.zeros(logits.shape, jnp.int32)
    tg = jnp.zeros(logits.shape, F32)
    rk = jnp.zeros(logits.shape, jnp.int32)
    for k in range(TOP_K):
        rank_k = jnp.sum(jnp.where(sels[k], before, 0.0), axis=1, keepdims=True)
        te = jnp.where(lane == k, idxs[k], te)
        tg = jnp.where(lane == k, exps[k] / denom, tg)
        rk = jnp.where(lane == k, rank_k.astype(jnp.int32), rk)
    te_ref[...] = te
    tg_ref[...] = tg
    rk_ref[...] = rk
    carry_ref[...] = carry_ref[...] + jnp.sum(onehot, axis=0, keepdims=True)
    cnt_ref[...] = carry_ref[...]


def _memory_attention(x1b, x1, kv, wq, wo, ln_g, ln_b, router_w, router_b, seq, n_mem, alpha, tm=256):
    t, d = x1.shape
    tm = min(tm, seq)
    ne = router_w.shape[1]
    assert ne <= LANES
    rw = jnp.zeros((d, LANES), F32).at[:, :ne].set(router_w.astype(F32))
    rwh = rw.astype(BF16)
    rwl = (rw - rwh.astype(F32)).astype(BF16)
    rb = jnp.full((1, LANES), NEG_BIG, F32).at[0, :ne].set(router_b.astype(F32))
    row = lambda a: a.reshape(1, -1).astype(F32)
    kt = kv[:, :d].reshape(t // seq, n_mem, d).transpose(0, 2, 1)
    tiles_per_seq = seq // tm
    tile_out = lambda w, dt: (pl.BlockSpec((tm, w), lambda i: (i, 0)), jax.ShapeDtypeStruct((t, w), dt))
    rh = d // 2 // LANES
    x2p_out = (pl.BlockSpec((tm * rh, LANES), lambda i: (i, 0)),
               jax.ShapeDtypeStruct((t * rh, LANES), jnp.uint32))
    outs = [tile_out(d, F32), x2p_out, tile_out(LANES, jnp.int32),
            tile_out(LANES, F32), tile_out(LANES, jnp.int32),
            (pl.BlockSpec((1, LANES), lambda i: (0, 0)), jax.ShapeDtypeStruct((1, LANES), F32))]
    return pl.pallas_call(
        functools.partial(_memattn_kernel, tm=tm, alpha=alpha),
        grid=(t // tm,),
        in_specs=[
            pl.BlockSpec((tm, d), lambda i: (i, 0)),
            pl.BlockSpec((tm, d), lambda i: (i, 0)),
            _resident((d, d), lambda i: (0, 0)),
            pl.BlockSpec((1, d, n_mem), lambda i: (i // tiles_per_seq, 0, 0)),
            pl.BlockSpec((n_mem, d), lambda i: (i // tiles_per_seq, 1)),
            _resident((d, d), lambda i: (0, 0)),
            _resident((1, d), lambda i: (0, 0)),
            _resident((1, d), lambda i: (0, 0)),
            _resident((d, 2 * LANES), lambda i: (0, 0)),
            _resident((1, LANES), lambda i: (0, 0)),
        ],
        out_specs=[o[0] for o in outs],
        out_shape=[o[1] for o in outs],
        scratch_shapes=[pltpu.VMEM((1, LANES), F32)],
        compiler_params=_cparams(("arbitrary",)),
    )(x1b, x1, wq.astype(BF16), kt, kv, wo.astype(BF16), row(ln_g), row(ln_b),
      jnp.concatenate([rwh, rwl], axis=1), rb)


def _dispatch_kernel(dest_ref, fill_ref, x2p_ref, xs_ref, zero_ref, stage_ref, sem, zsem,
                     *, rows, tm, n_fill, rh, n_steps):
    i = pl.program_id(0)

    @pl.when(i == 0)
    def _():
        zero_ref[...] = jnp.zeros(zero_ref.shape, zero_ref.dtype)

        def fill(n):
            start = pl.multiple_of(jnp.maximum(fill_ref[n], 0) * (tm * rh), tm * rh)
            return pltpu.make_async_copy(zero_ref, xs_ref.at[pl.ds(start, tm * rh)], zsem)

        for n in range(n_fill):
            @pl.when(fill_ref[n] >= 0)
            def _(n=n):
                fill(n).start()
        for n in range(n_fill):
            @pl.when(fill_ref[n] >= 0)
            def _(n=n):
                fill(n).wait()

    base = i * rows * TOP_K
    slot = i % 2
    stage_ref[slot] = x2p_ref[...]

    def issue(r, carry):
        src = pl.multiple_of(r * rh, rh)
        for k in range(TOP_K):
            dst = pl.multiple_of(dest_ref[base + r * TOP_K + k] * rh, rh)
            pltpu.make_async_copy(stage_ref.at[slot, pl.ds(src, rh)], xs_ref.at[pl.ds(dst, rh)],
                                  sem.at[slot]).start(priority=k % 2)
        return carry

    lax.fori_loop(0, rows, issue, 0, unroll=ISSUE_UNROLL)

    def drain(s):
        for _ in range(TOP_K):
            pltpu.make_async_copy(x2p_ref, xs_ref.at[pl.ds(0, rows * rh)], sem.at[s]).wait()

    @pl.when(i > 0)
    def _():
        drain(1 - slot)

    @pl.when(i == n_steps - 1)
    def _():
        drain(slot)


def _dispatch(dest, fill_tiles, x2p, n_rows, tm, rh, rows=512):
    t = x2p.shape[0] // rh
    rows = min(rows, t)
    n_fill = fill_tiles.shape[0]
    grid_spec = pltpu.PrefetchScalarGridSpec(
        num_scalar_prefetch=2,
        grid=(t // rows,),
        in_specs=[pl.BlockSpec((rows * rh, LANES), lambda i, d, z: (i, 0))],
        out_specs=pl.BlockSpec(memory_space=pl.ANY),
        scratch_shapes=[pltpu.VMEM((tm * rh, LANES), x2p.dtype),
                        pltpu.VMEM((2, rows * rh, LANES), x2p.dtype),
                        pltpu.SemaphoreType.DMA((2,)), pltpu.SemaphoreType.DMA],
    )
    return pl.pallas_call(
        functools.partial(_dispatch_kernel, rows=rows, tm=tm, n_fill=n_fill, rh=rh,
                          n_steps=t // rows),
        grid_spec=grid_spec,
        out_shape=jax.ShapeDtypeStruct((n_rows * rh, LANES), x2p.dtype),
        compiler_params=_cparams(("arbitrary",)),
    )(dest, fill_tiles, x2p)


def _pack_rows(lo, hi):
    lo = lax.bitcast_convert_type(lo.astype(BF16).astype(F32), jnp.uint32)
    hi = lax.bitcast_convert_type(hi.astype(BF16).astype(F32), jnp.uint32)
    return (lo >> 16) | (hi & jnp.uint32(0xFFFF0000))


def _store_row_tiles(ref, words):
    rows, width = words.shape
    chunks = width // LANES
    for c in range(chunks):
        ref[pl.ds(c, rows, stride=chunks), :] = words[:, c * LANES:(c + 1) * LANES]


def _load_row_tiles(ref, chunks):
    rows = ref.shape[0] // chunks
    return jnp.concatenate([ref[pl.ds(c, rows, stride=chunks), :] for c in range(chunks)], axis=1)


def _unpack_rows(words):
    lo = lax.bitcast_convert_type(words << 16, F32)
    hi = lax.bitcast_convert_type(words & jnp.uint32(0xFFFF0000), F32)
    return lo, hi


def _expert_up_kernel(ie_ref, in_ref, irt_ref, ivalid_ref, ifirst_ref, xs_ref, wg_ref, wu_ref,
                      bg_ref, bu_ref, h_ref, wgb_ref, wub_ref):
    w = pl.program_id(0)

    @pl.when(ifirst_ref[w] == 1)
    def _():
        wgb_ref[...] = wg_ref[0].astype(BF16)
        wub_ref[...] = wu_ref[0].astype(BF16)

    @pl.when(ivalid_ref[w] == 1)
    def _():
        rh = wgb_ref.shape[0] // 2 // LANES
        lo, hi = (a.astype(BF16) for a in _unpack_rows(_load_row_tiles(xs_ref, rh)))
        half = lo.shape[1]
        gate = _dot(lo, wgb_ref[:half, :]) + _dot(hi, wgb_ref[half:, :]) + bg_ref[0]
        up = _dot(lo, wub_ref[:half, :]) + _dot(hi, wub_ref[half:, :]) + bu_ref[0]
        gate = jnp.minimum(gate, SWIGLU_LIMIT)
        up = jnp.clip(up, -SWIGLU_LIMIT, SWIGLU_LIMIT)
        act = gate * jax.nn.sigmoid(SWIGLU_ALPHA * gate) * (up + 1.0)
        h_ref[...] = act.astype(h_ref.dtype)

    @pl.when(ivalid_ref[w] == 0)
    def _():
        h_ref[...] = jnp.zeros(h_ref.shape, h_ref.dtype)


def _expert_down_kernel(ie_ref, in_ref, irt_ref, ivalid_ref, ifirst_ref, h_ref, wd_ref, bd_ref,
                        o_ref, wdb_ref):
    w = pl.program_id(0)

    @pl.when(ifirst_ref[w] == 1)
    def _():
        wdb_ref[...] = wd_ref[0].astype(BF16)

    @pl.when(ivalid_ref[w] == 1)
    def _():
        out = _dot(h_ref[...], wdb_ref[...]) + bd_ref[0]
        half = out.shape[1] // 2
        _store_row_tiles(o_ref, _pack_rows(out[:, :half], out[:, half:]))

    @pl.when(ivalid_ref[w] == 0)
    def _():
        o_ref[...] = jnp.zeros(o_ref.shape, o_ref.dtype)


def _work_items(tiles_e, n_tiles_max, n_col):
    ne = tiles_e.shape[0]
    cum = jnp.cumsum(tiles_e)
    total_tiles = cum[-1]
    first_tile = cum - tiles_e
    item_end = cum * n_col
    w = jnp.arange(n_tiles_max * n_col, dtype=jnp.int32)
    e = jnp.minimum(jnp.sum(w[:, None] >= item_end[None, :], axis=1), ne - 1).astype(jnp.int32)
    te = jnp.maximum(tiles_e[e], 1)
    local = w - (item_end[e] - tiles_e[e] * n_col)
    valid = w < total_tiles * n_col
    spare = jnp.maximum(n_tiles_max - total_tiles, 1)
    j = w - total_tiles * n_col
    col = jnp.where(valid, local // te, j // spare)
    rt = jnp.where(valid, first_tile[e] + local % te, total_tiles + j % spare)
    first = jnp.where(valid, (local % te) == 0, False)
    i32 = lambda a: a.astype(jnp.int32)
    return i32(e), i32(col), i32(rt), i32(valid), i32(first)


def _expert_up(items, xs, w_gate, w_up, b_gate, b_up, tm, tn):
    ne, d, f = w_gate.shape
    rh = d // 2 // LANES
    n_rows = xs.shape[0] // rh
    tn = min(tn, f)
    n_items = items[0].shape[0]
    wspec = pl.BlockSpec((1, d, tn), lambda w, ie, ic, irt, iv, ifi: (ie[w], 0, ic[w]))
    bspec = pl.BlockSpec((1, 1, tn), lambda w, ie, ic, irt, iv, ifi: (ie[w], 0, ic[w]))
    grid_spec = pltpu.PrefetchScalarGridSpec(
        num_scalar_prefetch=5,
        grid=(n_items,),
        in_specs=[pl.BlockSpec((tm * rh, LANES), lambda w, ie, ic, irt, iv, ifi: (irt[w], 0)),
                  wspec, wspec, bspec, bspec],
        out_specs=pl.BlockSpec((tm, tn), lambda w, ie, ic, irt, iv, ifi: (irt[w], ic[w])),
        scratch_shapes=[pltpu.VMEM((d, tn), BF16), pltpu.VMEM((d, tn), BF16)],
    )
    return pl.pallas_call(
        _expert_up_kernel,
        grid_spec=grid_spec,
        out_shape=jax.ShapeDtypeStruct((n_rows, f), BF16),
        compiler_params=_cparams(("arbitrary",)),
    )(*items, xs, w_gate, w_up, b_gate.reshape(ne, 1, f), b_up.reshape(ne, 1, f))


def _expert_down(items, h, w_down, b_down, tm):
    n_rows, f = h.shape
    ne, _, d = w_down.shape
    rh = d // 2 // LANES
    n_items = items[0].shape[0]
    grid_spec = pltpu.PrefetchScalarGridSpec(
        num_scalar_prefetch=5,
        grid=(n_items,),
        in_specs=[pl.BlockSpec((tm, f), lambda w, ie, ic, irt, iv, ifi: (irt[w], 0)),
                  pl.BlockSpec((1, f, d), lambda w, ie, ic, irt, iv, ifi: (ie[w], 0, 0)),
                  pl.BlockSpec((1, 1, d), lambda w, ie, ic, irt, iv, ifi: (ie[w], 0, 0))],
        out_specs=pl.BlockSpec((tm * rh, LANES), lambda w, ie, ic, irt, iv, ifi: (irt[w], 0)),
        scratch_shapes=[pltpu.VMEM((f, d), BF16)],
    )
    return pl.pallas_call(
        _expert_down_kernel,
        grid_spec=grid_spec,
        out_shape=jax.ShapeDtypeStruct((n_rows * rh, LANES), jnp.uint32),
        compiler_params=_cparams(("arbitrary",)),
    )(*items, h, w_down, b_down.reshape(ne, 1, d))


def _combine_kernel(dest_ref, os_ref, tg_ref, x2_ref, g_ref, b_ref, o_ref, buf_ref, sem,
                    *, rows, alpha, rh, n_steps):
    i = pl.program_id(0)
    slot = i % 2

    def gather(step, into):
        base = step * rows * TOP_K

        def issue(r, carry):
            dst = pl.multiple_of(r * rh, rh)
            for k in range(TOP_K):
                src = pl.multiple_of(dest_ref[base + r * TOP_K + k] * rh, rh)
                pltpu.make_async_copy(os_ref.at[pl.ds(src, rh)], buf_ref.at[into, k, pl.ds(dst, rh)],
                                      sem.at[into]).start(priority=k % 2)
            return carry

        lax.fori_loop(0, rows, issue, 0, unroll=ISSUE_UNROLL)

    @pl.when(i == 0)
    def _():
        gather(0, 0)

    @pl.when(i + 1 < n_steps)
    def _():
        gather(i + 1, 1 - slot)

    for k in range(TOP_K):
        pltpu.make_async_copy(os_ref.at[pl.ds(0, rows * rh)], buf_ref.at[slot, k],
                              sem.at[slot]).wait()

    tg = tg_ref[...]
    lo_sum = hi_sum = None
    for k in range(TOP_K):
        lo, hi = _unpack_rows(_load_row_tiles(buf_ref.at[slot, k], rh))
        gate = tg[:, k:k + 1]
        lo_sum = gate * lo if k == 0 else lo_sum + gate * lo
        hi_sum = gate * hi if k == 0 else hi_sum + gate * hi
    y = jnp.concatenate([lo_sum, hi_sum], axis=1)
    o_ref[...] = _layer_norm(alpha * x2_ref[...] + y, g_ref[...], b_ref[...])


def _combine(dest, out_sorted, tg, x2, ln_g, ln_b, alpha, rows=256):
    t, d = x2.shape
    rows = min(rows, t)
    rh = d // 2 // LANES
    row = lambda a: a.reshape(1, -1).astype(F32)
    grid_spec = pltpu.PrefetchScalarGridSpec(
        num_scalar_prefetch=1,
        grid=(t // rows,),
        in_specs=[pl.BlockSpec(memory_space=pl.ANY),
                  pl.BlockSpec((rows, LANES), lambda i, s: (i, 0)),
                  pl.BlockSpec((rows, d), lambda i, s: (i, 0)),
                  pl.BlockSpec((1, d), lambda i, s: (0, 0)),
                  pl.BlockSpec((1, d), lambda i, s: (0, 0))],
        out_specs=pl.BlockSpec((rows, d), lambda i, s: (i, 0)),
        scratch_shapes=[pltpu.VMEM((2, TOP_K, rows * rh, LANES), jnp.uint32),
                        pltpu.SemaphoreType.DMA((2,))],
    )
    return pl.pallas_call(
        functools.partial(_combine_kernel, rows=rows, alpha=alpha, rh=rh, n_steps=t // rows),
        grid_spec=grid_spec,
        out_shape=jax.ShapeDtypeStruct((t, d), F32),
        compiler_params=_cparams(("arbitrary",)),
    )(dest, out_sorted, tg, x2, row(ln_g), row(ln_b))


def _moe(x2, x2p, te, tg, rk, counts, w_gate, b_gate, w_up, b_up, w_down, b_down, ln_g, ln_b,
         alpha, tm=512, tn_up=1024):
    t, d = x2.shape
    ne = w_gate.shape[0]
    n_assign = t * TOP_K
    tm = min(tm, t)
    n_tiles_max = -(-n_assign // tm) + ne
    n_rows = n_tiles_max * tm

    counts = counts[0, :ne].astype(jnp.int32)
    tiles_e = (counts + tm - 1) // tm
    pad_start = (jnp.cumsum(tiles_e) - tiles_e) * tm
    hit = te[:, :TOP_K, None] == jnp.arange(ne, dtype=jnp.int32)
    dest = (jnp.sum(jnp.where(hit, pad_start, 0), axis=-1) + rk[:, :TOP_K]).reshape(n_assign)

    cum_tiles = jnp.cumsum(tiles_e)
    last_tile = jnp.where(tiles_e > 0, cum_tiles - 1, -1)
    tail = cum_tiles[-1] + jnp.arange(ne, dtype=jnp.int32)
    fill_tiles = jnp.concatenate([last_tile, jnp.where(tail < n_tiles_max, tail, -1)])

    xs = _dispatch(dest, fill_tiles.astype(jnp.int32), x2p, n_rows, tm, d // 2 // LANES)
    f = w_gate.shape[2]
    tn_up = min(tn_up, f)
    items_up = _work_items(tiles_e, n_tiles_max, f // tn_up)
    h = _expert_up(items_up, xs, w_gate, w_up, b_gate, b_up, tm, tn_up)
    items_dn = _work_items(tiles_e, n_tiles_max, 1)
    out_sorted = _expert_down(items_dn, h, w_down, b_down, tm)
    return _combine(dest, out_sorted, tg, x2, ln_g, ln_b, alpha)


def kernel(x, mem, w_in, conv_w, attn_subln_w, lambda_q1, lambda_k1, lambda_q2, lambda_k2, w_out, ln1_g, ln1_b, mem_wq, mem_wkv, mem_wo, ln2_g, ln2_b, router_w, router_b, w_gate, b_gate, w_up, b_up, w_down, b_down, ln3_g, ln3_b):
    bsz, seq, d = x.shape
    n_mem = mem.shape[1]
    depth = w_in.shape[0]
    alpha = (2.0 * depth) ** 0.25
    slopes = jnp.exp2(-8.0 / N_ATTN_HEADS * jnp.arange(1, N_ATTN_HEADS + 1, dtype=F32))
    xf = x.reshape(bsz * seq, d)
    memf = mem.reshape(bsz * n_mem, d)
    for l in range(depth):
        lambda_init = 0.8 - 0.6 * math.exp(-0.3 * l)
        proj = _projection(xf, w_in[l].astype(BF16), 1024, 1024)
        oattn = _diff_attention(proj.reshape(bsz, seq, -1), slopes, lambda_q1[l], lambda_k1[l],
                                lambda_q2[l], lambda_k2[l], attn_subln_w[l], lambda_init)
        x1, x1b = _out_projection(oattn.reshape(bsz * seq, -1), proj, xf, conv_w[l], w_out[l],
                                  ln1_g[l], ln1_b[l], seq, alpha)
        kv = _projection(memf, mem_wkv[l].astype(BF16), 1024, 1024)
        x2, x2p, te, tg, rk, counts = _memory_attention(
            x1b, x1, kv, mem_wq[l], mem_wo[l], ln2_g[l], ln2_b[l], router_w[l], router_b[l],
            seq, n_mem, alpha)
        xf = _moe(x2, x2p, te, tg, rk, counts, w_gate[l], b_gate[l], w_up[l], b_up[l],
                  w_down[l], b_down[l], ln3_g[l], ln3_b[l], alpha)
    return xf.reshape(bsz, seq, d)
```

```python
import functools
import math

import jax
import jax.numpy as jnp
from jax import lax
from jax.experimental import pallas as pl
from jax.experimental.pallas import tpu as pltpu

N_ATTN_HEADS = 8
DIFF_HEAD_DIM = 64
HEAD_WIDTH = 2 * DIFF_HEAD_DIM
MEM_HEADS = 4
TOP_K = 4
CONV_TAPS = 3
SWIGLU_LIMIT = 7.0
SWIGLU_ALPHA = 1.702
LN_EPS = 1e-5

LANES = 128
SUBLANES = 8
VMEM_LIMIT = 56 * 1024 * 1024
NEG_BIG = -1e30
ALIBI_SPLIT = 16
ALIBI_QUERY_OFFSET_ROW = 5
QUERY_BLOCK = 256
ONES_ROWS = 16
UNDERFLOW_NATS = 104.0
NORM_SLACK = 1.01
ISSUE_UNROLL = 8

F32 = jnp.float32
BF16 = jnp.bfloat16


def _cparams(sem, vmem=VMEM_LIMIT):
    return pltpu.CompilerParams(dimension_semantics=sem, vmem_limit_bytes=vmem)


def _resident(shape, index_map):
    return pl.BlockSpec(shape, index_map, pipeline_mode=pl.Buffered(1))


def _layer_norm(z, g, b):
    mu = jnp.mean(z, axis=-1, keepdims=True)
    d = z - mu
    var = jnp.mean(d * d, axis=-1, keepdims=True)
    return d * lax.rsqrt(var + LN_EPS) * g + b


def _dot(a, b):
    return jnp.dot(a, b, preferred_element_type=F32)


def _proj_kernel(x_ref, w_ref, o_ref, xb_ref):
    @pl.when(pl.program_id(1) == 0)
    def _():
        xb_ref[...] = x_ref[...].astype(BF16)

    o_ref[...] = _dot(xb_ref[...], w_ref[...]).astype(o_ref.dtype)


def _projection(x, w, tm, tn):
    m, k = x.shape
    n = w.shape[1]
    tm, tn = min(tm, m), min(tn, n)
    return pl.pallas_call(
        _proj_kernel,
        grid=(m // tm, n // tn),
        in_specs=[pl.BlockSpec((tm, k), lambda i, j: (i, 0)),
                  pl.BlockSpec((k, tn), lambda i, j: (0, j))],
        out_specs=pl.BlockSpec((tm, tn), lambda i, j: (i, j)),
        out_shape=jax.ShapeDtypeStruct((m, n), BF16),
        scratch_shapes=[pltpu.VMEM((tm, k), BF16)],
        compiler_params=_cparams(("parallel", "arbitrary")),
    )(x, w)


def _attn_kernel(slopes_ref, q_ref, k_ref, v_ref, qft_ref, kf_ref, bdt_ref, lq1_ref, lk1_ref, lq2_ref,
                 lk2_ref, sub_ref, o_ref, lhst_ref, vt_ref, st_ref, pt_ref, m_ref, mx_ref, alpha_ref,
                 acct_ref, kn_ref, kaug_ref, *, tq, tk, seq, lambda_init):
    h = pl.program_id(1)
    qi = pl.program_id(2)
    slope = slopes_ref[h]

    @pl.when(qi == 0)
    def _():
        half_lane = lax.broadcasted_iota(jnp.int32, (tk, HEAD_WIDTH), 1) < DIFF_HEAD_DIM
        kn1 = jnp.zeros((1, 1), F32)
        kn2 = jnp.zeros((1, 1), F32)
        for c in range(seq // tk):
            rows_c = slice(c * tk, (c + 1) * tk)
            vt_ref[c, :HEAD_WIDTH, :] = v_ref[0, rows_c, :].astype(F32).T.astype(BF16)
            vt_ref[c, HEAD_WIDTH:, :] = jnp.ones((ONES_ROWS, tk), BF16)
            kaug_ref[c] = jnp.concatenate([k_ref[0, rows_c, :], kf_ref[0, rows_c, :]], axis=1)
            kc = k_ref[0, rows_c, :].astype(F32)
            sq = kc * kc
            n1 = jnp.sum(jnp.where(half_lane, sq, 0.0), axis=1, keepdims=True)
            n2 = jnp.sum(jnp.where(half_lane, 0.0, sq), axis=1, keepdims=True)
            kn1 = jnp.maximum(kn1, jnp.max(n1, axis=0, keepdims=True))
            kn2 = jnp.maximum(kn2, jnp.max(n2, axis=0, keepdims=True))
        kn_ref[...] = jnp.where(lax.broadcasted_iota(jnp.int32, kn_ref.shape, 1) == 0, kn1, kn2)

    qt = (q_ref[0].astype(F32) * (DIFF_HEAD_DIM ** -0.5)).T
    row = lax.broadcasted_iota(jnp.int32, qt.shape, 0)

    qsq = qt * qt
    qn1 = jnp.max(jnp.sum(jnp.where(row < DIFF_HEAD_DIM, qsq, 0.0), axis=0, keepdims=True),
                  axis=1, keepdims=True)
    qn2 = jnp.max(jnp.sum(jnp.where(row >= DIFF_HEAD_DIM, qsq, 0.0), axis=0, keepdims=True),
                  axis=1, keepdims=True)
    kn = kn_ref[...]
    qk = jnp.sqrt(jnp.maximum(qn1 * kn[:, 0:1], qn2 * kn[:, 1:2]))
    reach = jnp.minimum((UNDERFLOW_NATS + 2.0 * NORM_SLACK * qk) / slope, 4.0 * seq)
    q_lo = (qi * tq).astype(F32)
    lo_f = jnp.floor((q_lo - (tk - 1) - reach) / tk) + 1.0
    hi_f = jnp.ceil((q_lo + (tq - 1) + reach) / tk)
    n_tiles = seq // tk
    per_q = tq // tk
    lo = jnp.clip(lo_f.astype(jnp.int32)[0, 0], 0, qi * per_q)
    hi = jnp.clip(hi_f.astype(jnp.int32)[0, 0], (qi + 1) * per_q, n_tiles)
    if n_tiles > 1:
        short = hi - lo < 2
        lo, hi = (jnp.where(short & (lo > 0), lo - 1, lo), jnp.where(short & (lo == 0), hi + 1, hi))
    qqt = jnp.concatenate([jnp.where(row < DIFF_HEAD_DIM, qt, 0.0),
                           jnp.where(row >= DIFF_HEAD_DIM, qt, 0.0)], axis=1).astype(BF16)
    qft = qft_ref[0]
    offset = jnp.full(qft.shape, -slope * (qi * tq).astype(F32), F32).astype(BF16)
    frow = lax.broadcasted_iota(jnp.int32, qft.shape, 0)
    qft = jnp.where(frow == ALIBI_QUERY_OFFSET_ROW, offset, qft)
    fft = jnp.concatenate([qft, qft], axis=1)
    lhst_ref[0] = jnp.concatenate([qqt, fft], axis=0)
    lhst_ref[1] = jnp.concatenate([qqt, -fft], axis=0)
    lhst_ref[2] = jnp.concatenate([qqt, jnp.zeros_like(fft)], axis=0)

    m_ref[...] = jnp.full(m_ref.shape, NEG_BIG, F32)
    acct_ref[...] = jnp.zeros(acct_ref.shape, F32)

    blocks = [slice(qb * QUERY_BLOCK, (qb + 1) * QUERY_BLOCK)
              for qb in range(2 * tq // QUERY_BLOCK)]

    def scores(kt):
        overlap = kt - qi * per_q
        variant = jnp.where(overlap < 0, 0, jnp.where(overlap >= per_q, 1, 2))
        k_aug = kaug_ref[kt]
        for cols in blocks:
            st = _dot(k_aug, lhst_ref[variant, :, cols])
            st_ref[:, cols] = st
            mx_ref[:, cols] = jnp.max(st, axis=0, keepdims=True)

        @pl.when(variant == 2)
        def _():
            bias_t = bdt_ref[0, jnp.clip(overlap, 0, per_q - 1)]
            for cols in blocks:
                di = cols.start % tq
                st = st_ref[:, cols] + bias_t[:, di:di + QUERY_BLOCK]
                st_ref[:, cols] = st
                mx_ref[:, cols] = jnp.max(st, axis=0, keepdims=True)

    def probs():
        for cols in blocks:
            st = st_ref[:, cols]
            m_prev = m_ref[:, cols]
            m_new = jnp.maximum(m_prev, mx_ref[:, cols])
            alpha_ref[:, cols] = jnp.exp(m_prev - m_new)
            pt_ref[:, cols] = jnp.exp(st - m_new).astype(BF16)
            m_ref[:, cols] = m_new

    def accumulate(kt):
        vt = vt_ref[kt]
        for cols in blocks:
            acct_ref[:, cols] = alpha_ref[:, cols] * acct_ref[:, cols] + _dot(vt, pt_ref[:, cols])

    def step(j, carry):
        accumulate(j - 2)
        probs()
        scores(j)
        return carry

    scores(lo)
    if n_tiles > 1:
        probs()
        scores(lo + 1)
        lax.fori_loop(lo + 2, hi, step, 0)
        accumulate(hi - 2)
    probs()
    accumulate(hi - 1)

    lam = (jnp.exp(jnp.sum(lq1_ref[...] * lk1_ref[...], keepdims=True))
           - jnp.exp(jnp.sum(lq2_ref[...] * lk2_ref[...], keepdims=True)) + lambda_init)
    acct = acct_ref[...]
    ot = acct[:HEAD_WIDTH, :] / acct[HEAD_WIDTH:HEAD_WIDTH + 1, :]
    ot = ot[:, :tq] - lam * ot[:, tq:]
    ms = jnp.mean(ot * ot, axis=0, keepdims=True)
    yt = ot * lax.rsqrt(ms + LN_EPS) * sub_ref[...] * (1.0 - lambda_init)
    o_ref[0] = yt.T.astype(o_ref.dtype)


def _alibi_features(slopes, tq, tk, seq):
    nh = slopes.shape[0]
    sl = slopes[:, None]

    def split(n, t):
        pos = jnp.arange(n, dtype=jnp.int32) % t
        return ((pos // ALIBI_SPLIT * ALIBI_SPLIT).astype(F32)[None, :],
                (pos % ALIBI_SPLIT).astype(F32)[None, :])

    qhi, qlo = split(tq, tq)
    khi, klo = split(seq, tk)
    tile_start = (jnp.arange(seq, dtype=jnp.int32) // tk * tk).astype(F32)[None, :]
    qone, kone = jnp.ones((nh, tq), F32), jnp.ones((nh, seq), F32)
    assert ALIBI_QUERY_OFFSET_ROW == 5
    qf = jnp.stack([qone, qone, -sl * qhi, -sl * qlo, qone, 0 * qone]
                   + [0 * qone] * (HEAD_WIDTH - 6), axis=-1)
    kf = jnp.stack([sl * khi, sl * klo, kone, kone, sl * tile_start, kone]
                   + [0 * kone] * (HEAD_WIDTH - 6), axis=-1)
    return qf.astype(BF16), kf.astype(BF16)


def _diff_attention(proj3, slopes, lq1, lk1, lq2, lk2, subln, lambda_init, tq=512, tk=512):
    bsz, seq, _ = proj3.shape
    nh = N_ATTN_HEADS
    tq, tk = min(tq, seq), min(tk, seq)
    assert tq % tk == 0 and seq % tq == 0 and (2 * tq) % QUERY_BLOCK == 0
    qf, kf = _alibi_features(slopes, tq, tk, seq)
    qft = qf.transpose(0, 2, 1)
    di = jnp.arange(tq, dtype=jnp.int32)[None, None, :]
    dj = jnp.arange(tk, dtype=jnp.int32)[None, :, None]
    r = jnp.arange(tq // tk, dtype=jnp.int32)[:, None, None]
    bias_t = -slopes[:, None, None, None] * jnp.abs(di - dj - r * tk).astype(F32)[None]
    vec = lambda a: a.reshape(1, -1).astype(F32)
    small = lambda n: pl.BlockSpec((1, n), lambda b, h, i, s: (0, 0))
    grid_spec = pltpu.PrefetchScalarGridSpec(
        num_scalar_prefetch=1,
        grid=(bsz, nh, seq // tq),
        in_specs=[
            pl.BlockSpec((1, tq, HEAD_WIDTH), lambda b, h, i, s: (b, i, h)),
            pl.BlockSpec((1, seq, HEAD_WIDTH), lambda b, h, i, s: (b, 0, nh + h)),
            pl.BlockSpec((1, seq, HEAD_WIDTH), lambda b, h, i, s: (b, 0, 2 * nh + h)),
            pl.BlockSpec((1, HEAD_WIDTH, tq), lambda b, h, i, s: (h, 0, 0)),
            pl.BlockSpec((1, seq, HEAD_WIDTH), lambda b, h, i, s: (h, 0, 0)),
            pl.BlockSpec((1, tq // tk, tk, tq), lambda b, h, i, s: (h, 0, 0, 0)),
            small(DIFF_HEAD_DIM), small(DIFF_HEAD_DIM), small(DIFF_HEAD_DIM), small(DIFF_HEAD_DIM),
            pl.BlockSpec((HEAD_WIDTH, 1), lambda b, h, i, s: (0, 0)),
        ],
        out_specs=pl.BlockSpec((1, tq, HEAD_WIDTH), lambda b, h, i, s: (b, i, h)),
        scratch_shapes=[pltpu.VMEM((3, 2 * HEAD_WIDTH, 2 * tq), BF16),
                        pltpu.VMEM((seq // tk, HEAD_WIDTH + ONES_ROWS, tk), BF16),
                        pltpu.VMEM((tk, 2 * tq), F32),
                        pltpu.VMEM((tk, 2 * tq), BF16),
                        pltpu.VMEM((1, 2 * tq), F32),
                        pltpu.VMEM((1, 2 * tq), F32),
                        pltpu.VMEM((1, 2 * tq), F32),
                        pltpu.VMEM((HEAD_WIDTH + ONES_ROWS, 2 * tq), F32),
                        pltpu.VMEM((1, LANES), F32),
                        pltpu.VMEM((seq // tk, tk, 2 * HEAD_WIDTH), BF16)],
    )
    return pl.pallas_call(
        functools.partial(_attn_kernel, tq=tq, tk=tk, seq=seq, lambda_init=lambda_init),
        grid_spec=grid_spec,
        out_shape=jax.ShapeDtypeStruct((bsz, seq, nh * HEAD_WIDTH), BF16),
        compiler_params=_cparams(("parallel", "parallel", "arbitrary")),
    )(slopes, proj3, proj3, proj3, qft, kf, bias_t,
      vec(lq1), vec(lk1), vec(lq2), vec(lk2), subln.reshape(-1, 1).astype(F32))


def _outproj_kernel(oa_ref, gb_ref, gc_ref, u_ref, cp_ref, up_ref, cn_ref, un_ref, cw_ref,
                    wa_ref, wc_ref, x_ref, g_ref, b_ref, x1_ref, x1b_ref, *, tm, seq, alpha):
    i = pl.program_id(0)
    cu = gc_ref[...].astype(F32) * u_ref[...].astype(F32)
    row = lax.broadcasted_iota(jnp.int32, cu.shape, 0)
    last = SUBLANES - 1
    prev_edge = cp_ref[last:last + 1, :].astype(F32) * up_ref[last:last + 1, :].astype(F32)
    next_edge = cn_ref[0:1, :].astype(F32) * un_ref[0:1, :].astype(F32)
    prev_edge = jnp.where((i * tm) % seq == 0, 0.0, prev_edge)
    next_edge = jnp.where(((i + 1) * tm) % seq == 0, 0.0, next_edge)
    cu_prev = jnp.where(row == 0, prev_edge, pltpu.roll(cu, 1, 0))
    cu_next = jnp.where(row == tm - 1, next_edge, pltpu.roll(cu, tm - 1, 0))
    cw = cw_ref[...]
    conv = cu_prev * cw[0:1, :] + cu * cw[1:2, :] + cu_next * cw[2:3, :]
    oc = (gb_ref[...].astype(F32) * conv).astype(BF16)
    y = _dot(oa_ref[...], wa_ref[...]) + _dot(oc, wc_ref[...])
    x1 = _layer_norm(alpha * x_ref[...] + y, g_ref[...], b_ref[...])
    x1_ref[...] = x1
    x1b_ref[...] = x1.astype(BF16)


def _out_projection(oattn, proj, x, conv_w, w_out, ln_g, ln_b, seq, alpha, tm=512):
    t, d = x.shape
    aw = oattn.shape[1]
    cwid = d - aw
    assert aw == cwid and proj.shape[1] == 3 * aw + 3 * cwid
    tm = min(tm, seq)
    hb = tm // SUBLANES
    nhb = t // SUBLANES
    cb = 3 * aw // cwid
    prev_map = lambda c: (lambda i: (jnp.maximum(i * hb - 1, 0), c))
    next_map = lambda c: (lambda i: (jnp.minimum((i + 1) * hb, nhb - 1), c))
    wa = w_out[:aw].astype(BF16)
    wc = w_out[aw:].astype(BF16)
    row = lambda a: a.reshape(1, -1).astype(F32)
    return pl.pallas_call(
        functools.partial(_outproj_kernel, tm=tm, seq=seq, alpha=alpha),
        grid=(t // tm,),
        in_specs=[
            pl.BlockSpec((tm, aw), lambda i: (i, 0)),
            pl.BlockSpec((tm, cwid), lambda i: (i, cb)),
            pl.BlockSpec((tm, cwid), lambda i: (i, cb + 1)),
            pl.BlockSpec((tm, cwid), lambda i: (i, cb + 2)),
            pl.BlockSpec((SUBLANES, cwid), prev_map(cb + 1)),
            pl.BlockSpec((SUBLANES, cwid), prev_map(cb + 2)),
            pl.BlockSpec((SUBLANES, cwid), next_map(cb + 1)),
            pl.BlockSpec((SUBLANES, cwid), next_map(cb + 2)),
            _resident((CONV_TAPS, cwid), lambda i: (0, 0)),
            _resident((aw, d), lambda i: (0, 0)),
            _resident((cwid, d), lambda i: (0, 0)),
            pl.BlockSpec((tm, d), lambda i: (i, 0)),
            _resident((1, d), lambda i: (0, 0)),
            _resident((1, d), lambda i: (0, 0)),
        ],
        out_specs=[pl.BlockSpec((tm, d), lambda i: (i, 0)),
                   pl.BlockSpec((tm, d), lambda i: (i, 0))],
        out_shape=[jax.ShapeDtypeStruct((t, d), F32), jax.ShapeDtypeStruct((t, d), BF16)],
        compiler_params=_cparams(("parallel",)),
    )(oattn, proj, proj, proj, proj, proj, proj, proj, conv_w.astype(F32), wa, wc, x,
      row(ln_g), row(ln_b))


def _memattn_kernel(x1b_ref, x1_ref, wq_ref, kt_ref, v_ref, wo_ref, g_ref, b_ref, rw_ref, rb_ref,
                    x2_ref, x2p_ref, te_ref, tg_ref, rk_ref, cnt_ref, carry_ref, *, tm, alpha):
    i = pl.program_id(0)
    d = x1_ref.shape[1]
    hd = d // MEM_HEADS

    @pl.when(i == 0)
    def _():
        carry_ref[...] = jnp.zeros(carry_ref.shape, F32)

    q = _dot(x1b_ref[...], wq_ref[...]).astype(BF16)
    scale = hd ** -0.5
    heads = []
    for hh in range(MEM_HEADS):
        cols = slice(hh * hd, (hh + 1) * hd)
        s = _dot(q[:, cols], kt_ref[0, cols, :]) * scale
        p = jnp.exp(s - jnp.max(s, axis=1, keepdims=True))
        p = p / jnp.sum(p, axis=1, keepdims=True)
        heads.append(_dot(p.astype(BF16), v_ref[:, cols]).astype(BF16))
    xa = _dot(jnp.concatenate(heads, axis=1), wo_ref[...])
    x2 = _layer_norm(alpha * x1_ref[...] + xa, g_ref[...], b_ref[...])
    x2_ref[...] = x2

    _store_row_tiles(x2p_ref, _pack_rows(x2[:, :d // 2], x2[:, d // 2:]))

    xh = x2.astype(BF16)
    xl = (x2 - xh.astype(F32)).astype(BF16)
    both = _dot(xh, rw_ref[...])
    logits = both[:, :LANES] + both[:, LANES:] + _dot(xl, rw_ref[:, :LANES]) + rb_ref[...]

    lane = lax.broadcasted_iota(jnp.int32, logits.shape, 1)
    work = logits
    vals, sels, idxs = [], [], []
    for _ in range(TOP_K):
        mk = jnp.max(work, axis=1, keepdims=True)
        idx = jnp.min(jnp.where(work == mk, lane, LANES), axis=1, keepdims=True)
        sel = lane == idx
        vals.append(mk)
        idxs.append(idx)
        sels.append(sel)
        work = jnp.where(sel, -jnp.inf, work)
    exps = [jnp.exp(v - vals[0]) for v in vals]
    denom = exps[0]
    for e in exps[1:]:
        denom = denom + e

    onehot = jnp.zeros(logits.shape, F32)
    for sel in sels:
        onehot = onehot + sel.astype(F32)
    lower = (lax.broadcasted_iota(jnp.int32, (tm, tm), 0)
             > lax.broadcasted_iota(jnp.int32, (tm, tm), 1)).astype(BF16)
    before = _dot(lower, onehot.astype(BF16)) + carry_ref[...]

    te = jnp.zeros(logits.shape, jnp.int32)
    tg = jnp.zeros(logits.shape, F32)
    rk = jnp.zeros(logits.shape, jnp.int32)
    for k in range(TOP_K):
        rank_k = jnp.sum(jnp.where(sels[k], before, 0.0), axis=1, keepdims=True)
        te = jnp.where(lane == k, idxs[k], te)
        tg = jnp.where(lane == k, exps[k] / denom, tg)
        rk = jnp.where(lane == k, rank_k.astype(jnp.int32), rk)
    te_ref[...] = te
    tg_ref[...] = tg
    rk_ref[...] = rk
    carry_ref[...] = carry_ref[...] + jnp.sum(onehot, axis=0, keepdims=True)
    cnt_ref[...] = carry_ref[...]


def _memory_attention(x1b, x1, kv, wq, wo, ln_g, ln_b, router_w, router_b, seq, n_mem, alpha, tm=256):
    t, d = x1.shape
    tm = min(tm, seq)
    ne = router_w.shape[1]
    assert ne <= LANES
    rw = jnp.zeros((d, LANES), F32).at[:, :ne].set(router_w.astype(F32))
    rwh = rw.astype(BF16)
    rwl = (rw - rwh.astype(F32)).astype(BF16)
    rb = jnp.full((1, LANES), NEG_BIG, F32).at[0, :ne].set(router_b.astype(F32))
    row = lambda a: a.reshape(1, -1).astype(F32)
    kt = kv[:, :d].reshape(t // seq, n_mem, d).transpose(0, 2, 1)
    tiles_per_seq = seq // tm
    tile_out = lambda w, dt: (pl.BlockSpec((tm, w), lambda i: (i, 0)), jax.ShapeDtypeStruct((t, w), dt))
    rh = d // 2 // LANES
    x2p_out = (pl.BlockSpec((tm * rh, LANES), lambda i: (i, 0)),
               jax.ShapeDtypeStruct((t * rh, LANES), jnp.uint32))
    outs = [tile_out(d, F32), x2p_out, tile_out(LANES, jnp.int32),
            tile_out(LANES, F32), tile_out(LANES, jnp.int32),
            (pl.BlockSpec((1, LANES), lambda i: (0, 0)), jax.ShapeDtypeStruct((1, LANES), F32))]
    return pl.pallas_call(
        functools.partial(_memattn_kernel, tm=tm, alpha=alpha),
        grid=(t // tm,),
        in_specs=[
            pl.BlockSpec((tm, d), lambda i: (i, 0)),
            pl.BlockSpec((tm, d), lambda i: (i, 0)),
            _resident((d, d), lambda i: (0, 0)),
            pl.BlockSpec((1, d, n_mem), lambda i: (i // tiles_per_seq, 0, 0)),
            pl.BlockSpec((n_mem, d), lambda i: (i // tiles_per_seq, 1)),
            _resident((d, d), lambda i: (0, 0)),
            _resident((1, d), lambda i: (0, 0)),
            _resident((1, d), lambda i: (0, 0)),
            _resident((d, 2 * LANES), lambda i: (0, 0)),
            _resident((1, LANES), lambda i: (0, 0)),
        ],
        out_specs=[o[0] for o in outs],
        out_shape=[o[1] for o in outs],
        scratch_shapes=[pltpu.VMEM((1, LANES), F32)],
        compiler_params=_cparams(("arbitrary",)),
    )(x1b, x1, wq.astype(BF16), kt, kv, wo.astype(BF16), row(ln_g), row(ln_b),
      jnp.concatenate([rwh, rwl], axis=1), rb)


def _dispatch_kernel(dest_ref, fill_ref, x2p_ref, xs_ref, zero_ref, stage_ref, sem, zsem,
                     *, rows, tm, n_fill, rh, n_steps):
    i = pl.program_id(0)

    @pl.when(i == 0)
    def _():
        zero_ref[...] = jnp.zeros(zero_ref.shape, zero_ref.dtype)

        def fill(n):
            start = pl.multiple_of(jnp.maximum(fill_ref[n], 0) * (tm * rh), tm * rh)
            return pltpu.make_async_copy(zero_ref, xs_ref.at[pl.ds(start, tm * rh)], zsem)

        for n in range(n_fill):
            @pl.when(fill_ref[n] >= 0)
            def _(n=n):
                fill(n).start()
        for n in range(n_fill):
            @pl.when(fill_ref[n] >= 0)
            def _(n=n):
                fill(n).wait()

    base = i * rows * TOP_K
    slot = i % 2
    stage_ref[slot] = x2p_ref[...]

    def issue(r, carry):
        src = pl.multiple_of(r * rh, rh)
        for k in range(TOP_K):
            dst = pl.multiple_of(dest_ref[base + r * TOP_K + k] * rh, rh)
            pltpu.make_async_copy(stage_ref.at[slot, pl.ds(src, rh)], xs_ref.at[pl.ds(dst, rh)],
                                  sem.at[slot]).start(priority=k % 2)
        return carry

    lax.fori_loop(0, rows, issue, 0, unroll=ISSUE_UNROLL)

    def drain(s):
        for _ in range(TOP_K):
            pltpu.make_async_copy(x2p_ref, xs_ref.at[pl.ds(0, rows * rh)], sem.at[s]).wait()

    @pl.when(i > 0)
    def _():
        drain(1 - slot)

    @pl.when(i == n_steps - 1)
    def _():
        drain(slot)


def _dispatch(dest, fill_tiles, x2p, n_rows, tm, rh, rows=512):
    t = x2p.shape[0] // rh
    rows = min(rows, t)
    n_fill = fill_tiles.shape[0]
    grid_spec = pltpu.PrefetchScalarGridSpec(
        num_scalar_prefetch=2,
        grid=(t // rows,),
        in_specs=[pl.BlockSpec((rows * rh, LANES), lambda i, d, z: (i, 0))],
        out_specs=pl.BlockSpec(memory_space=pl.ANY),
        scratch_shapes=[pltpu.VMEM((tm * rh, LANES), x2p.dtype),
                        pltpu.VMEM((2, rows * rh, LANES), x2p.dtype),
                        pltpu.SemaphoreType.DMA((2,)), pltpu.SemaphoreType.DMA],
    )
    return pl.pallas_call(
        functools.partial(_dispatch_kernel, rows=rows, tm=tm, n_fill=n_fill, rh=rh,
                          n_steps=t // rows),
        grid_spec=grid_spec,
        out_shape=jax.ShapeDtypeStruct((n_rows * rh, LANES), x2p.dtype),
        compiler_params=_cparams(("arbitrary",)),
    )(dest, fill_tiles, x2p)


def _pack_rows(lo, hi):
    lo = lax.bitcast_convert_type(lo.astype(BF16).astype(F32), jnp.uint32)
    hi = lax.bitcast_convert_type(hi.astype(BF16).astype(F32), jnp.uint32)
    return (lo >> 16) | (hi & jnp.uint32(0xFFFF0000))


def _store_row_tiles(ref, words):
    rows, width = words.shape
    chunks = width // LANES
    for c in range(chunks):
        ref[pl.ds(c, rows, stride=chunks), :] = words[:, c * LANES:(c + 1) * LANES]


def _load_row_tiles(ref, chunks):
    rows = ref.shape[0] // chunks
    return jnp.concatenate([ref[pl.ds(c, rows, stride=chunks), :] for c in range(chunks)], axis=1)


def _unpack_rows(words):
    lo = lax.bitcast_convert_type(words << 16, F32)
    hi = lax.bitcast_convert_type(words & jnp.uint32(0xFFFF0000), F32)
    return lo, hi


def _expert_up_kernel(ie_ref, in_ref, irt_ref, ivalid_ref, ifirst_ref, xs_ref, wg_ref, wu_ref,
                      bg_ref, bu_ref, h_ref, wgb_ref, wub_ref):
    w = pl.program_id(0)

    @pl.when(ifirst_ref[w] == 1)
    def _():
        wgb_ref[...] = wg_ref[0].astype(BF16)
        wub_ref[...] = wu_ref[0].astype(BF16)

    @pl.when(ivalid_ref[w] == 1)
    def _():
        rh = wgb_ref.shape[0] // 2 // LANES
        lo, hi = (a.astype(BF16) for a in _unpack_rows(_load_row_tiles(xs_ref, rh)))
        half = lo.shape[1]
        gate = _dot(lo, wgb_ref[:half, :]) + _dot(hi, wgb_ref[half:, :]) + bg_ref[0]
        up = _dot(lo, wub_ref[:half, :]) + _dot(hi, wub_ref[half:, :]) + bu_ref[0]
        gate = jnp.minimum(gate, SWIGLU_LIMIT)
        up = jnp.clip(up, -SWIGLU_LIMIT, SWIGLU_LIMIT)
        act = gate * jax.nn.sigmoid(SWIGLU_ALPHA * gate) * (up + 1.0)
        h_ref[...] = act.astype(h_ref.dtype)

    @pl.when(ivalid_ref[w] == 0)
    def _():
        h_ref[...] = jnp.zeros(h_ref.shape, h_ref.dtype)


def _expert_down_kernel(ie_ref, in_ref, irt_ref, ivalid_ref, ifirst_ref, h_ref, wd_ref, bd_ref,
                        o_ref, wdb_ref):
    w = pl.program_id(0)

    @pl.when(ifirst_ref[w] == 1)
    def _():
        wdb_ref[...] = wd_ref[0].astype(BF16)

    @pl.when(ivalid_ref[w] == 1)
    def _():
        out = _dot(h_ref[...], wdb_ref[...]) + bd_ref[0]
        half = out.shape[1] // 2
        _store_row_tiles(o_ref, _pack_rows(out[:, :half], out[:, half:]))

    @pl.when(ivalid_ref[w] == 0)
    def _():
        o_ref[...] = jnp.zeros(o_ref.shape, o_ref.dtype)


def _work_items(tiles_e, n_tiles_max, n_col):
    ne = tiles_e.shape[0]
    cum = jnp.cumsum(tiles_e)
    total_tiles = cum[-1]
    first_tile = cum - tiles_e
    item_end = cum * n_col
    w = jnp.arange(n_tiles_max * n_col, dtype=jnp.int32)
    e = jnp.minimum(jnp.sum(w[:, None] >= item_end[None, :], axis=1), ne - 1).astype(jnp.int32)
    te = jnp.maximum(tiles_e[e], 1)
    local = w - (item_end[e] - tiles_e[e] * n_col)
    valid = w < total_tiles * n_col
    spare = jnp.maximum(n_tiles_max - total_tiles, 1)
    j = w - total_tiles * n_col
    col = jnp.where(valid, local // te, j // spare)
    rt = jnp.where(valid, first_tile[e] + local % te, total_tiles + j % spare)
    first = jnp.where(valid, (local % te) == 0, False)
    i32 = lambda a: a.astype(jnp.int32)
    return i32(e), i32(col), i32(rt), i32(valid), i32(first)


def _expert_up(items, xs, w_gate, w_up, b_gate, b_up, tm, tn):
    ne, d, f = w_gate.shape
    rh = d // 2 // LANES
    n_rows = xs.shape[0] // rh
    tn = min(tn, f)
    n_items = items[0].shape[0]
    wspec = pl.BlockSpec((1, d, tn), lambda w, ie, ic, irt, iv, ifi: (ie[w], 0, ic[w]))
    bspec = pl.BlockSpec((1, 1, tn), lambda w, ie, ic, irt, iv, ifi: (ie[w], 0, ic[w]))
    grid_spec = pltpu.PrefetchScalarGridSpec(
        num_scalar_prefetch=5,
        grid=(n_items,),
        in_specs=[pl.BlockSpec((tm * rh, LANES), lambda w, ie, ic, irt, iv, ifi: (irt[w], 0)),
                  wspec, wspec, bspec, bspec],
        out_specs=pl.BlockSpec((tm, tn), lambda w, ie, ic, irt, iv, ifi: (irt[w], ic[w])),
        scratch_shapes=[pltpu.VMEM((d, tn), BF16), pltpu.VMEM((d, tn), BF16)],
    )
    return pl.pallas_call(
        _expert_up_kernel,
        grid_spec=grid_spec,
        out_shape=jax.ShapeDtypeStruct((n_rows, f), BF16),
        compiler_params=_cparams(("arbitrary",)),
    )(*items, xs, w_gate, w_up, b_gate.reshape(ne, 1, f), b_up.reshape(ne, 1, f))


def _expert_down(items, h, w_down, b_down, tm):
    n_rows, f = h.shape
    ne, _, d = w_down.shape
    rh = d // 2 // LANES
    n_items = items[0].shape[0]
    grid_spec = pltpu.PrefetchScalarGridSpec(
        num_scalar_prefetch=5,
        grid=(n_items,),
        in_specs=[pl.BlockSpec((tm, f), lambda w, ie, ic, irt, iv, ifi: (irt[w], 0)),
                  pl.BlockSpec((1, f, d), lambda w, ie, ic, irt, iv, ifi: (ie[w], 0, 0)),
                  pl.BlockSpec((1, 1, d), lambda w, ie, ic, irt, iv, ifi: (ie[w], 0, 0))],
        out_specs=pl.BlockSpec((tm * rh, LANES), lambda w, ie, ic, irt, iv, ifi: (irt[w], 0)),
        scratch_shapes=[pltpu.VMEM((f, d), BF16)],
    )
    return pl.pallas_call(
        _expert_down_kernel,
        grid_spec=grid_spec,
        out_shape=jax.ShapeDtypeStruct((n_rows * rh, LANES), jnp.uint32),
        compiler_params=_cparams(("arbitrary",)),
    )(*items, h, w_down, b_down.reshape(ne, 1, d))


def _combine_kernel(dest_ref, os_ref, tg_ref, x2_ref, g_ref, b_ref, o_ref, buf_ref, sem,
                    *, rows, alpha, rh, n_steps):
    i = pl.program_id(0)
    slot = i % 2

    def gather(step, into):
        base = step * rows * TOP_K

        def issue(r, carry):
            dst = pl.multiple_of(r * rh, rh)
            for k in range(TOP_K):
                src = pl.multiple_of(dest_ref[base + r * TOP_K + k] * rh, rh)
                pltpu.make_async_copy(os_ref.at[pl.ds(src, rh)], buf_ref.at[into, k, pl.ds(dst, rh)],
                                      sem.at[into]).start(priority=k % 2)
            return carry

        lax.fori_loop(0, rows, issue, 0, unroll=ISSUE_UNROLL)

    @pl.when(i == 0)
    def _():
        gather(0, 0)

    @pl.when(i + 1 < n_steps)
    def _():
        gather(i + 1, 1 - slot)

    for k in range(TOP_K):
        pltpu.make_async_copy(os_ref.at[pl.ds(0, rows * rh)], buf_ref.at[slot, k],
                              sem.at[slot]).wait()

    tg = tg_ref[...]
    lo_sum = hi_sum = None
    for k in range(TOP_K):
        lo, hi = _unpack_rows(_load_row_tiles(buf_ref.at[slot, k], rh))
        gate = tg[:, k:k + 1]
        lo_sum = gate * lo if k == 0 else lo_sum + gate * lo
        hi_sum = gate * hi if k == 0 else hi_sum + gate * hi
    y = jnp.concatenate([lo_sum, hi_sum], axis=1)
    o_ref[...] = _layer_norm(alpha * x2_ref[...] + y, g_ref[...], b_ref[...])


def _combine(dest, out_sorted, tg, x2, ln_g, ln_b, alpha, rows=256):
    t, d = x2.shape
    rows = min(rows, t)
    rh = d // 2 // LANES
    row = lambda a: a.reshape(1, -1).astype(F32)
    grid_spec = pltpu.PrefetchScalarGridSpec(
        num_scalar_prefetch=1,
        grid=(t // rows,),
        in_specs=[pl.BlockSpec(memory_space=pl.ANY),
                  pl.BlockSpec((rows, LANES), lambda i, s: (i, 0)),
                  pl.BlockSpec((rows, d), lambda i, s: (i, 0)),
                  pl.BlockSpec((1, d), lambda i, s: (0, 0)),
                  pl.BlockSpec((1, d), lambda i, s: (0, 0))],
        out_specs=pl.BlockSpec((rows, d), lambda i, s: (i, 0)),
        scratch_shapes=[pltpu.VMEM((2, TOP_K, rows * rh, LANES), jnp.uint32),
                        pltpu.SemaphoreType.DMA((2,))],
    )
    return pl.pallas_call(
        functools.partial(_combine_kernel, rows=rows, alpha=alpha, rh=rh, n_steps=t // rows),
        grid_spec=grid_spec,
        out_shape=jax.ShapeDtypeStruct((t, d), F32),
        compiler_params=_cparams(("arbitrary",)),
    )(dest, out_sorted, tg, x2, row(ln_g), row(ln_b))


def _moe(x2, x2p, te, tg, rk, counts, w_gate, b_gate, w_up, b_up, w_down, b_down, ln_g, ln_b,
         alpha, tm=512, tn_up=1024):
    t, d = x2.shape
    ne = w_gate.shape[0]
    n_assign = t * TOP_K
    tm = min(tm, t)
    n_tiles_max = -(-n_assign // tm) + ne
    n_rows = n_tiles_max * tm

    counts = counts[0, :ne].astype(jnp.int32)
    tiles_e = (counts + tm - 1) // tm
    pad_start = (jnp.cumsum(tiles_e) - tiles_e) * tm
    hit = te[:, :TOP_K, None] == jnp.arange(ne, dtype=jnp.int32)
    dest = (jnp.sum(jnp.where(hit, pad_start, 0), axis=-1) + rk[:, :TOP_K]).reshape(n_assign)

    cum_tiles = jnp.cumsum(tiles_e)
    last_tile = jnp.where(tiles_e > 0, cum_tiles - 1, -1)
    tail = cum_tiles[-1] + jnp.arange(ne, dtype=jnp.int32)
    fill_tiles = jnp.concatenate([last_tile, jnp.where(tail < n_tiles_max, tail, -1)])

    xs = _dispatch(dest, fill_tiles.astype(jnp.int32), x2p, n_rows, tm, d // 2 // LANES)
    f = w_gate.shape[2]
    tn_up = min(tn_up, f)
    items_up = _work_items(tiles_e, n_tiles_max, f // tn_up)
    h = _expert_up(items_up, xs, w_gate, w_up, b_gate, b_up, tm, tn_up)
    items_dn = _work_items(tiles_e, n_tiles_max, 1)
    out_sorted = _expert_down(items_dn, h, w_down, b_down, tm)
    return _combine(dest, out_sorted, tg, x2, ln_g, ln_b, alpha)


def kernel(x, mem, w_in, conv_w, attn_subln_w, lambda_q1, lambda_k1, lambda_q2, lambda_k2, w_out, ln1_g, ln1_b, mem_wq, mem_wkv, mem_wo, ln2_g, ln2_b, router_w, router_b, w_gate, b_gate, w_up, b_up, w_down, b_down, ln3_g, ln3_b):
    bsz, seq, d = x.shape
    n_mem = mem.shape[1]
    depth = w_in.shape[0]
    alpha = (2.0 * depth) ** 0.25
    slopes = jnp.exp2(-8.0 / N_ATTN_HEADS * jnp.arange(1, N_ATTN_HEADS + 1, dtype=F32))
    xf = x.reshape(bsz * seq, d)
    memf = mem.reshape(bsz * n_mem, d)
    for l in range(depth):
        lambda_init = 0.8 - 0.6 * math.exp(-0.3 * l)
        proj = _projection(xf, w_in[l].astype(BF16), 1024, 1024)
        oattn = _diff_attention(proj.reshape(bsz, seq, -1), slopes, lambda_q1[l], lambda_k1[l],
                                lambda_q2[l], lambda_k2[l], attn_subln_w[l], lambda_init)
        x1, x1b = _out_projection(oattn.reshape(bsz * seq, -1), proj, xf, conv_w[l], w_out[l],
                                  ln1_g[l], ln1_b[l], seq, alpha)
        kv = _projection(memf, mem_wkv[l].astype(BF16), 1024, 1024)
        x2, x2p, te, tg, rk, counts = _memory_attention(
            x1b, x1, kv, mem_wq[l], mem_wo[l], ln2_g[l], ln2_b[l], router_w[l], router_b[l],
            seq, n_mem, alpha)
        xf = _moe(x2, x2p, te, tg, rk, counts, w_gate[l], b_gate[l], w_up[l], b_up[l],
                  w_down[l], b_down[l], ln3_g[l], ln3_b[l], alpha)
    return xf.reshape(bsz, seq, d)
```

```python
import functools
import math

import jax
import jax.numpy as jnp
from jax import lax
from jax.experimental import pallas as pl
from jax.experimental.pallas import tpu as pltpu

N_ATTN_HEADS = 8
DIFF_HEAD_DIM = 64
HEAD_WIDTH = 2 * DIFF_HEAD_DIM
MEM_HEADS = 4
TOP_K = 4
CONV_TAPS = 3
SWIGLU_LIMIT = 7.0
SWIGLU_ALPHA = 1.702
LN_EPS = 1e-5

LANES = 128
SUBLANES = 8
VMEM_LIMIT = 56 * 1024 * 1024
NEG_BIG = -1e30
ALIBI_SPLIT = 16
ALIBI_OFFSET_LANE = 4
QUERY_BLOCK = 256
ONES_ROWS = 16
UNDERFLOW_NATS = 104.0
NORM_SLACK = 1.01
ISSUE_UNROLL = 8

F32 = jnp.float32
BF16 = jnp.bfloat16


def _cparams(sem, vmem=VMEM_LIMIT):
    return pltpu.CompilerParams(dimension_semantics=sem, vmem_limit_bytes=vmem)


def _resident(shape, index_map):
    return pl.BlockSpec(shape, index_map, pipeline_mode=pl.Buffered(1))


def _layer_norm(z, g, b):
    mu = jnp.mean(z, axis=-1, keepdims=True)
    d = z - mu
    var = jnp.mean(d * d, axis=-1, keepdims=True)
    return d * lax.rsqrt(var + LN_EPS) * g + b


def _dot(a, b):
    return jnp.dot(a, b, preferred_element_type=F32)


def _proj_kernel(x_ref, w_ref, o_ref, xb_ref):
    @pl.when(pl.program_id(1) == 0)
    def _():
        xb_ref[...] = x_ref[...].astype(BF16)

    o_ref[...] = _dot(xb_ref[...], w_ref[...]).astype(o_ref.dtype)


def _projection(x, w, tm, tn):
    m, k = x.shape
    n = w.shape[1]
    tm, tn = min(tm, m), min(tn, n)
    return pl.pallas_call(
        _proj_kernel,
        grid=(m // tm, n // tn),
        in_specs=[pl.BlockSpec((tm, k), lambda i, j: (i, 0)),
                  pl.BlockSpec((k, tn), lambda i, j: (0, j))],
        out_specs=pl.BlockSpec((tm, tn), lambda i, j: (i, j)),
        out_shape=jax.ShapeDtypeStruct((m, n), BF16),
        scratch_shapes=[pltpu.VMEM((tm, k), BF16)],
        compiler_params=_cparams(("parallel", "arbitrary")),
    )(x, w)


def _attn_kernel(slopes_ref, q_ref, k_ref, v_ref, qft_ref, kf_ref, bdt_ref, lq1_ref, lk1_ref, lq2_ref,
                 lk2_ref, sub_ref, o_ref, lhst_ref, vt_ref, st_ref, pt_ref, m_ref, mx_ref, alpha_ref,
                 acct_ref, kn_ref, *, tq, tk, seq, lambda_init):
    h = pl.program_id(1)
    qi = pl.program_id(2)
    slope = slopes_ref[h]

    @pl.when(qi == 0)
    def _():
        half_lane = lax.broadcasted_iota(jnp.int32, (tk, HEAD_WIDTH), 1) < DIFF_HEAD_DIM
        kn1 = jnp.zeros((1, 1), F32)
        kn2 = jnp.zeros((1, 1), F32)
        for c in range(seq // tk):
            vt_ref[c, :HEAD_WIDTH, :] = v_ref[0, c * tk:(c + 1) * tk, :].astype(F32).T.astype(BF16)
            vt_ref[c, HEAD_WIDTH:, :] = jnp.ones((ONES_ROWS, tk), BF16)
            kc = k_ref[0, c * tk:(c + 1) * tk, :].astype(F32)
            sq = kc * kc
            n1 = jnp.sum(jnp.where(half_lane, sq, 0.0), axis=1, keepdims=True)
            n2 = jnp.sum(jnp.where(half_lane, 0.0, sq), axis=1, keepdims=True)
            kn1 = jnp.maximum(kn1, jnp.max(n1, axis=0, keepdims=True))
            kn2 = jnp.maximum(kn2, jnp.max(n2, axis=0, keepdims=True))
        kn_ref[...] = jnp.where(lax.broadcasted_iota(jnp.int32, kn_ref.shape, 1) == 0, kn1, kn2)

    qt = (q_ref[0].astype(F32) * (DIFF_HEAD_DIM ** -0.5)).T
    row = lax.broadcasted_iota(jnp.int32, qt.shape, 0)

    qsq = qt * qt
    qn1 = jnp.max(jnp.sum(jnp.where(row < DIFF_HEAD_DIM, qsq, 0.0), axis=0, keepdims=True),
                  axis=1, keepdims=True)
    qn2 = jnp.max(jnp.sum(jnp.where(row >= DIFF_HEAD_DIM, qsq, 0.0), axis=0, keepdims=True),
                  axis=1, keepdims=True)
    kn = kn_ref[...]
    qk = jnp.sqrt(jnp.maximum(qn1 * kn[:, 0:1], qn2 * kn[:, 1:2]))
    reach = jnp.minimum((UNDERFLOW_NATS + 2.0 * NORM_SLACK * qk) / slope, 4.0 * seq)
    q_lo = (qi * tq).astype(F32)
    lo_f = jnp.floor((q_lo - (tk - 1) - reach) / tk) + 1.0
    hi_f = jnp.ceil((q_lo + (tq - 1) + reach) / tk)
    n_tiles = seq // tk
    per_q = tq // tk
    lo = jnp.clip(lo_f.astype(jnp.int32)[0, 0], 0, qi * per_q)
    hi = jnp.clip(hi_f.astype(jnp.int32)[0, 0], (qi + 1) * per_q, n_tiles)
    if n_tiles > 1:
        short = hi - lo < 2
        lo, hi = (jnp.where(short & (lo > 0), lo - 1, lo), jnp.where(short & (lo == 0), hi + 1, hi))
    qqt = jnp.concatenate([jnp.where(row < DIFF_HEAD_DIM, qt, 0.0),
                           jnp.where(row >= DIFF_HEAD_DIM, qt, 0.0)], axis=1).astype(BF16)
    fft = jnp.concatenate([qft_ref[0], qft_ref[0]], axis=1)
    lhst_ref[0] = jnp.concatenate([qqt, fft], axis=0)
    lhst_ref[1] = jnp.concatenate([qqt, -fft], axis=0)
    lhst_ref[2] = jnp.concatenate([qqt, jnp.zeros_like(fft)], axis=0)

    m_ref[...] = jnp.full(m_ref.shape, NEG_BIG, F32)
    acct_ref[...] = jnp.zeros(acct_ref.shape, F32)
    kf = kf_ref[0]
    klane = lax.broadcasted_iota(jnp.int32, (tk, HEAD_WIDTH), 1)

    blocks = [slice(qb * QUERY_BLOCK, (qb + 1) * QUERY_BLOCK)
              for qb in range(2 * tq // QUERY_BLOCK)]

    def scores(kt):
        start = pl.multiple_of(kt * tk, tk)
        overlap = kt - qi * per_q
        variant = jnp.where(overlap < 0, 0, jnp.where(overlap >= per_q, 1, 2))
        c = slope * (kt * tk - qi * tq).astype(F32)
        feat = jnp.where(klane == ALIBI_OFFSET_LANE, jnp.full(kf.shape, c, F32).astype(BF16), kf)
        k_aug = jnp.concatenate([k_ref[0, pl.ds(start, tk), :], feat], axis=1)
        for cols in blocks:
            st = _dot(k_aug, lhst_ref[variant, :, cols])
            st_ref[:, cols] = st
            mx_ref[:, cols] = jnp.max(st, axis=0, keepdims=True)

        @pl.when(variant == 2)
        def _():
            bias_t = bdt_ref[0, jnp.clip(overlap, 0, per_q - 1)]
            for cols in blocks:
                di = cols.start % tq
                st = st_ref[:, cols] + bias_t[:, di:di + QUERY_BLOCK]
                st_ref[:, cols] = st
                mx_ref[:, cols] = jnp.max(st, axis=0, keepdims=True)

    def probs():
        for cols in blocks:
            st = st_ref[:, cols]
            m_prev = m_ref[:, cols]
            m_new = jnp.maximum(m_prev, mx_ref[:, cols])
            alpha_ref[:, cols] = jnp.exp(m_prev - m_new)
            pt_ref[:, cols] = jnp.exp(st - m_new).astype(BF16)
            m_ref[:, cols] = m_new

    def accumulate(kt):
        vt = vt_ref[kt]
        for cols in blocks:
            acct_ref[:, cols] = alpha_ref[:, cols] * acct_ref[:, cols] + _dot(vt, pt_ref[:, cols])

    def step(j, carry):
        accumulate(j - 2)
        probs()
        scores(j)
        return carry

    scores(lo)
    if n_tiles > 1:
        probs()
        scores(lo + 1)
        lax.fori_loop(lo + 2, hi, step, 0)
        accumulate(hi - 2)
    probs()
    accumulate(hi - 1)

    lam = (jnp.exp(jnp.sum(lq1_ref[...] * lk1_ref[...], keepdims=True))
           - jnp.exp(jnp.sum(lq2_ref[...] * lk2_ref[...], keepdims=True)) + lambda_init)
    acct = acct_ref[...]
    ot = acct[:HEAD_WIDTH, :] / acct[HEAD_WIDTH:HEAD_WIDTH + 1, :]
    ot = ot[:, :tq] - lam * ot[:, tq:]
    ms = jnp.mean(ot * ot, axis=0, keepdims=True)
    yt = ot * lax.rsqrt(ms + LN_EPS) * sub_ref[...] * (1.0 - lambda_init)
    o_ref[0] = yt.T.astype(o_ref.dtype)


def _alibi_features(slopes, t):
    pos = jnp.arange(t, dtype=jnp.int32)
    hi = (pos // ALIBI_SPLIT * ALIBI_SPLIT).astype(F32)[None, :]
    lo = (pos % ALIBI_SPLIT).astype(F32)[None, :]
    sl = slopes[:, None]
    one = jnp.ones((slopes.shape[0], t), F32)
    zero = jnp.zeros_like(one)
    pad = [zero] * (HEAD_WIDTH - 5)
    qf = jnp.stack([one, one, -sl * hi, -sl * lo, one] + pad, axis=-1)
    kf = jnp.stack([sl * hi, sl * lo, one, one, zero] + pad, axis=-1)
    assert ALIBI_OFFSET_LANE == 4
    return qf.astype(BF16), kf.astype(BF16)


def _diff_attention(proj3, slopes, lq1, lk1, lq2, lk2, subln, lambda_init, tq=1024, tk=512):
    bsz, seq, _ = proj3.shape
    nh = N_ATTN_HEADS
    tq, tk = min(tq, seq), min(tk, seq)
    assert tq % tk == 0 and seq % tq == 0 and (2 * tq) % QUERY_BLOCK == 0
    qft = _alibi_features(slopes, tq)[0].transpose(0, 2, 1)
    kf = _alibi_features(slopes, tk)[1]
    di = jnp.arange(tq, dtype=jnp.int32)[None, None, :]
    dj = jnp.arange(tk, dtype=jnp.int32)[None, :, None]
    r = jnp.arange(tq // tk, dtype=jnp.int32)[:, None, None]
    bias_t = -slopes[:, None, None, None] * jnp.abs(di - dj - r * tk).astype(F32)[None]
    vec = lambda a: a.reshape(1, -1).astype(F32)
    small = lambda n: pl.BlockSpec((1, n), lambda b, h, i, s: (0, 0))
    grid_spec = pltpu.PrefetchScalarGridSpec(
        num_scalar_prefetch=1,
        grid=(bsz, nh, seq // tq),
        in_specs=[
            pl.BlockSpec((1, tq, HEAD_WIDTH), lambda b, h, i, s: (b, i, h)),
            pl.BlockSpec((1, seq, HEAD_WIDTH), lambda b, h, i, s: (b, 0, nh + h)),
            pl.BlockSpec((1, seq, HEAD_WIDTH), lambda b, h, i, s: (b, 0, 2 * nh + h)),
            pl.BlockSpec((1, HEAD_WIDTH, tq), lambda b, h, i, s: (h, 0, 0)),
            pl.BlockSpec((1, tk, HEAD_WIDTH), lambda b, h, i, s: (h, 0, 0)),
            pl.BlockSpec((1, tq // tk, tk, tq), lambda b, h, i, s: (h, 0, 0, 0)),
            small(DIFF_HEAD_DIM), small(DIFF_HEAD_DIM), small(DIFF_HEAD_DIM), small(DIFF_HEAD_DIM),
            pl.BlockSpec((HEAD_WIDTH, 1), lambda b, h, i, s: (0, 0)),
        ],
        out_specs=pl.BlockSpec((1, tq, HEAD_WIDTH), lambda b, h, i, s: (b, i, h)),
        scratch_shapes=[pltpu.VMEM((3, 2 * HEAD_WIDTH, 2 * tq), BF16),
                        pltpu.VMEM((seq // tk, HEAD_WIDTH + ONES_ROWS, tk), BF16),
                        pltpu.VMEM((tk, 2 * tq), F32),
                        pltpu.VMEM((tk, 2 * tq), BF16),
                        pltpu.VMEM((1, 2 * tq), F32),
                        pltpu.VMEM((1, 2 * tq), F32),
                        pltpu.VMEM((1, 2 * tq), F32),
                        pltpu.VMEM((HEAD_WIDTH + ONES_ROWS, 2 * tq), F32),
                        pltpu.VMEM((1, LANES), F32)],
    )
    return pl.pallas_call(
        functools.partial(_attn_kernel, tq=tq, tk=tk, seq=seq, lambda_init=lambda_init),
        grid_spec=grid_spec,
        out_shape=jax.ShapeDtypeStruct((bsz, seq, nh * HEAD_WIDTH), BF16),
        compiler_params=_cparams(("parallel", "parallel", "arbitrary")),
    )(slopes, proj3, proj3, proj3, qft, kf, bias_t,
      vec(lq1), vec(lk1), vec(lq2), vec(lk2), subln.reshape(-1, 1).astype(F32))


def _outproj_kernel(oa_ref, gb_ref, gc_ref, u_ref, cp_ref, up_ref, cn_ref, un_ref, cw_ref,
                    wa_ref, wc_ref, x_ref, g_ref, b_ref, x1_ref, x1b_ref, *, tm, seq, alpha):
    i = pl.program_id(0)
    cu = gc_ref[...].astype(F32) * u_ref[...].astype(F32)
    row = lax.broadcasted_iota(jnp.int32, cu.shape, 0)
    last = SUBLANES - 1
    prev_edge = cp_ref[last:last + 1, :].astype(F32) * up_ref[last:last + 1, :].astype(F32)
    next_edge = cn_ref[0:1, :].astype(F32) * un_ref[0:1, :].astype(F32)
    prev_edge = jnp.where((i * tm) % seq == 0, 0.0, prev_edge)
    next_edge = jnp.where(((i + 1) * tm) % seq == 0, 0.0, next_edge)
    cu_prev = jnp.where(row == 0, prev_edge, pltpu.roll(cu, 1, 0))
    cu_next = jnp.where(row == tm - 1, next_edge, pltpu.roll(cu, tm - 1, 0))
    cw = cw_ref[...]
    conv = cu_prev * cw[0:1, :] + cu * cw[1:2, :] + cu_next * cw[2:3, :]
    oc = (gb_ref[...].astype(F32) * conv).astype(BF16)
    y = _dot(oa_ref[...], wa_ref[...]) + _dot(oc, wc_ref[...])
    x1 = _layer_norm(alpha * x_ref[...] + y, g_ref[...], b_ref[...])
    x1_ref[...] = x1
    x1b_ref[...] = x1.astype(BF16)


def _out_projection(oattn, proj, x, conv_w, w_out, ln_g, ln_b, seq, alpha, tm=512):
    t, d = x.shape
    aw = oattn.shape[1]
    cwid = d - aw
    assert aw == cwid and proj.shape[1] == 3 * aw + 3 * cwid
    tm = min(tm, seq)
    hb = tm // SUBLANES
    nhb = t // SUBLANES
    cb = 3 * aw // cwid
    prev_map = lambda c: (lambda i: (jnp.maximum(i * hb - 1, 0), c))
    next_map = lambda c: (lambda i: (jnp.minimum((i + 1) * hb, nhb - 1), c))
    wa = w_out[:aw].astype(BF16)
    wc = w_out[aw:].astype(BF16)
    row = lambda a: a.reshape(1, -1).astype(F32)
    return pl.pallas_call(
        functools.partial(_outproj_kernel, tm=tm, seq=seq, alpha=alpha),
        grid=(t // tm,),
        in_specs=[
            pl.BlockSpec((tm, aw), lambda i: (i, 0)),
            pl.BlockSpec((tm, cwid), lambda i: (i, cb)),
            pl.BlockSpec((tm, cwid), lambda i: (i, cb + 1)),
            pl.BlockSpec((tm, cwid), lambda i: (i, cb + 2)),
            pl.BlockSpec((SUBLANES, cwid), prev_map(cb + 1)),
            pl.BlockSpec((SUBLANES, cwid), prev_map(cb + 2)),
            pl.BlockSpec((SUBLANES, cwid), next_map(cb + 1)),
            pl.BlockSpec((SUBLANES, cwid), next_map(cb + 2)),
            _resident((CONV_TAPS, cwid), lambda i: (0, 0)),
            _resident((aw, d), lambda i: (0, 0)),
            _resident((cwid, d), lambda i: (0, 0)),
            pl.BlockSpec((tm, d), lambda i: (i, 0)),
            _resident((1, d), lambda i: (0, 0)),
            _resident((1, d), lambda i: (0, 0)),
        ],
        out_specs=[pl.BlockSpec((tm, d), lambda i: (i, 0)),
                   pl.BlockSpec((tm, d), lambda i: (i, 0))],
        out_shape=[jax.ShapeDtypeStruct((t, d), F32), jax.ShapeDtypeStruct((t, d), BF16)],
        compiler_params=_cparams(("parallel",)),
    )(oattn, proj, proj, proj, proj, proj, proj, proj, conv_w.astype(F32), wa, wc, x,
      row(ln_g), row(ln_b))


def _memattn_kernel(x1b_ref, x1_ref, wq_ref, kt_ref, v_ref, wo_ref, g_ref, b_ref, rw_ref, rb_ref,
                    x2_ref, x2p_ref, te_ref, tg_ref, rk_ref, cnt_ref, carry_ref, *, tm, alpha):
    i = pl.program_id(0)
    d = x1_ref.shape[1]
    hd = d // MEM_HEADS

    @pl.when(i == 0)
    def _():
        carry_ref[...] = jnp.zeros(carry_ref.shape, F32)

    q = _dot(x1b_ref[...], wq_ref[...]).astype(BF16)
    scale = hd ** -0.5
    heads = []
    for hh in range(MEM_HEADS):
        cols = slice(hh * hd, (hh + 1) * hd)
        s = _dot(q[:, cols], kt_ref[0, cols, :]) * scale
        p = jnp.exp(s - jnp.max(s, axis=1, keepdims=True))
        p = p / jnp.sum(p, axis=1, keepdims=True)
        heads.append(_dot(p.astype(BF16), v_ref[:, cols]).astype(BF16))
    xa = _dot(jnp.concatenate(heads, axis=1), wo_ref[...])
    x2 = _layer_norm(alpha * x1_ref[...] + xa, g_ref[...], b_ref[...])
    x2_ref[...] = x2

    _store_row_tiles(x2p_ref, _pack_rows(x2[:, :d // 2], x2[:, d // 2:]))

    xh = x2.astype(BF16)
    xl = (x2 - xh.astype(F32)).astype(BF16)
    both = _dot(xh, rw_ref[...])
    logits = both[:, :LANES] + both[:, LANES:] + _dot(xl, rw_ref[:, :LANES]) + rb_ref[...]

    lane = lax.broadcasted_iota(jnp.int32, logits.shape, 1)
    work = logits
    vals, sels, idxs = [], [], []
    for _ in range(TOP_K):
        mk = jnp.max(work, axis=1, keepdims=True)
        idx = jnp.min(jnp.where(work == mk, lane, LANES), axis=1, keepdims=True)
        sel = lane == idx
        vals.append(mk)
        idxs.append(idx)
        sels.append(sel)
        work = jnp.where(sel, -jnp.inf, work)
    exps = [jnp.exp(v - vals[0]) for v in vals]
    denom = exps[0]
    for e in exps[1:]:
        denom = denom + e

    onehot = jnp.zeros(logits.shape, F32)
    for sel in sels:
        onehot = onehot + sel.astype(F32)
    lower = (lax.broadcasted_iota(jnp.int32, (tm, tm), 0)
             > lax.broadcasted_iota(jnp.int32, (tm, tm), 1)).astype(BF16)
    before = _dot(lower, onehot.astype(BF16)) + carry_ref[...]

    te = jnp.zeros(logits.shape, jnp.int32)
    tg = jnp.zeros(logits.shape, F32)
    rk = jnp.zeros(logits.shape, jnp.int32)
    for k in range(TOP_K):
        rank_k = jnp.sum(jnp.where(sels[k], before, 0.0), axis=1, keepdims=True)
        te = jnp.where(lane == k, idxs[k], te)
        tg = jnp.where(lane == k, exps[k] / denom, tg)
        rk = jnp.where(lane == k, rank_k.astype(jnp.int32), rk)
    te_ref[...] = te
    tg_ref[...] = tg
    rk_ref[...] = rk
    carry_ref[...] = carry_ref[...] + jnp.sum(onehot, axis=0, keepdims=True)
    cnt_ref[...] = carry_ref[...]


def _memory_attention(x1b, x1, kv, wq, wo, ln_g, ln_b, router_w, router_b, seq, n_mem, alpha, tm=256):
    t, d = x1.shape
    tm = min(tm, seq)
    ne = router_w.shape[1]
    assert ne <= LANES
    rw = jnp.zeros((d, LANES), F32).at[:, :ne].set(router_w.astype(F32))
    rwh = rw.astype(BF16)
    rwl = (rw - rwh.astype(F32)).astype(BF16)
    rb = jnp.full((1, LANES), NEG_BIG, F32).at[0, :ne].set(router_b.astype(F32))
    row = lambda a: a.reshape(1, -1).astype(F32)
    kt = kv[:, :d].reshape(t // seq, n_mem, d).transpose(0, 2, 1)
    tiles_per_seq = seq // tm
    tile_out = lambda w, dt: (pl.BlockSpec((tm, w), lambda i: (i, 0)), jax.ShapeDtypeStruct((t, w), dt))
    rh = d // 2 // LANES
    x2p_out = (pl.BlockSpec((tm * rh, LANES), lambda i: (i, 0)),
               jax.ShapeDtypeStruct((t * rh, LANES), jnp.uint32))
    outs = [tile_out(d, F32), x2p_out, tile_out(LANES, jnp.int32),
            tile_out(LANES, F32), tile_out(LANES, jnp.int32),
            (pl.BlockSpec((1, LANES), lambda i: (0, 0)), jax.ShapeDtypeStruct((1, LANES), F32))]
    return pl.pallas_call(
        functools.partial(_memattn_kernel, tm=tm, alpha=alpha),
        grid=(t // tm,),
        in_specs=[
            pl.BlockSpec((tm, d), lambda i: (i, 0)),
            pl.BlockSpec((tm, d), lambda i: (i, 0)),
            _resident((d, d), lambda i: (0, 0)),
            pl.BlockSpec((1, d, n_mem), lambda i: (i // tiles_per_seq, 0, 0)),
            pl.BlockSpec((n_mem, d), lambda i: (i // tiles_per_seq, 1)),
            _resident((d, d), lambda i: (0, 0)),
            _resident((1, d), lambda i: (0, 0)),
            _resident((1, d), lambda i: (0, 0)),
            _resident((d, 2 * LANES), lambda i: (0, 0)),
            _resident((1, LANES), lambda i: (0, 0)),
        ],
        out_specs=[o[0] for o in outs],
        out_shape=[o[1] for o in outs],
        scratch_shapes=[pltpu.VMEM((1, LANES), F32)],
        compiler_params=_cparams(("arbitrary",)),
    )(x1b, x1, wq.astype(BF16), kt, kv, wo.astype(BF16), row(ln_g), row(ln_b),
      jnp.concatenate([rwh, rwl], axis=1), rb)


def _dispatch_kernel(dest_ref, fill_ref, x2p_ref, xs_ref, zero_ref, stage_ref, sem, zsem,
                     *, rows, tm, n_fill, rh, n_steps):
    i = pl.program_id(0)

    @pl.when(i == 0)
    def _():
        zero_ref[...] = jnp.zeros(zero_ref.shape, zero_ref.dtype)

        def fill(n):
            start = pl.multiple_of(jnp.maximum(fill_ref[n], 0) * (tm * rh), tm * rh)
            return pltpu.make_async_copy(zero_ref, xs_ref.at[pl.ds(start, tm * rh)], zsem)

        for n in range(n_fill):
            @pl.when(fill_ref[n] >= 0)
            def _(n=n):
                fill(n).start()
        for n in range(n_fill):
            @pl.when(fill_ref[n] >= 0)
            def _(n=n):
                fill(n).wait()

    base = i * rows * TOP_K
    slot = i % 2
    stage_ref[slot] = x2p_ref[...]

    def issue(r, carry):
        src = pl.multiple_of(r * rh, rh)
        for k in range(TOP_K):
            dst = pl.multiple_of(dest_ref[base + r * TOP_K + k] * rh, rh)
            pltpu.make_async_copy(stage_ref.at[slot, pl.ds(src, rh)], xs_ref.at[pl.ds(dst, rh)],
                                  sem.at[slot]).start(priority=k % 2)
        return carry

    lax.fori_loop(0, rows, issue, 0, unroll=ISSUE_UNROLL)

    def drain(s):
        for _ in range(TOP_K):
            pltpu.make_async_copy(x2p_ref, xs_ref.at[pl.ds(0, rows * rh)], sem.at[s]).wait()

    @pl.when(i > 0)
    def _():
        drain(1 - slot)

    @pl.when(i == n_steps - 1)
    def _():
        drain(slot)


def _dispatch(dest, fill_tiles, x2p, n_rows, tm, rh, rows=512):
    t = x2p.shape[0] // rh
    rows = min(rows, t)
    n_fill = fill_tiles.shape[0]
    grid_spec = pltpu.PrefetchScalarGridSpec(
        num_scalar_prefetch=2,
        grid=(t // rows,),
        in_specs=[pl.BlockSpec((rows * rh, LANES), lambda i, d, z: (i, 0))],
        out_specs=pl.BlockSpec(memory_space=pl.ANY),
        scratch_shapes=[pltpu.VMEM((tm * rh, LANES), x2p.dtype),
                        pltpu.VMEM((2, rows * rh, LANES), x2p.dtype),
                        pltpu.SemaphoreType.DMA((2,)), pltpu.SemaphoreType.DMA],
    )
    return pl.pallas_call(
        functools.partial(_dispatch_kernel, rows=rows, tm=tm, n_fill=n_fill, rh=rh,
                          n_steps=t // rows),
        grid_spec=grid_spec,
        out_shape=jax.ShapeDtypeStruct((n_rows * rh, LANES), x2p.dtype),
        compiler_params=_cparams(("arbitrary",)),
    )(dest, fill_tiles, x2p)


def _pack_rows(lo, hi):
    lo = lax.bitcast_convert_type(lo.astype(BF16).astype(F32), jnp.uint32)
    hi = lax.bitcast_convert_type(hi.astype(BF16).astype(F32), jnp.uint32)
    return (lo >> 16) | (hi & jnp.uint32(0xFFFF0000))


def _store_row_tiles(ref, words):
    rows, width = words.shape
    chunks = width // LANES
    for c in range(chunks):
        ref[pl.ds(c, rows, stride=chunks), :] = words[:, c * LANES:(c + 1) * LANES]


def _load_row_tiles(ref, chunks):
    rows = ref.shape[0] // chunks
    return jnp.concatenate([ref[pl.ds(c, rows, stride=chunks), :] for c in range(chunks)], axis=1)


def _unpack_rows(words):
    lo = lax.bitcast_convert_type(words << 16, F32)
    hi = lax.bitcast_convert_type(words & jnp.uint32(0xFFFF0000), F32)
    return lo, hi


def _expert_up_kernel(ie_ref, in_ref, irt_ref, ivalid_ref, ifirst_ref, xs_ref, wg_ref, wu_ref,
                      bg_ref, bu_ref, h_ref, wgb_ref, wub_ref):
    w = pl.program_id(0)

    @pl.when(ifirst_ref[w] == 1)
    def _():
        wgb_ref[...] = wg_ref[0].astype(BF16)
        wub_ref[...] = wu_ref[0].astype(BF16)

    @pl.when(ivalid_ref[w] == 1)
    def _():
        rh = wgb_ref.shape[0] // 2 // LANES
        lo, hi = (a.astype(BF16) for a in _unpack_rows(_load_row_tiles(xs_ref, rh)))
        half = lo.shape[1]
        gate = _dot(lo, wgb_ref[:half, :]) + _dot(hi, wgb_ref[half:, :]) + bg_ref[0]
        up = _dot(lo, wub_ref[:half, :]) + _dot(hi, wub_ref[half:, :]) + bu_ref[0]
        gate = jnp.minimum(gate, SWIGLU_LIMIT)
        up = jnp.clip(up, -SWIGLU_LIMIT, SWIGLU_LIMIT)
        act = gate * jax.nn.sigmoid(SWIGLU_ALPHA * gate) * (up + 1.0)
        h_ref[...] = act.astype(h_ref.dtype)

    @pl.when(ivalid_ref[w] == 0)
    def _():
        h_ref[...] = jnp.zeros(h_ref.shape, h_ref.dtype)


def _expert_down_kernel(ie_ref, in_ref, irt_ref, ivalid_ref, ifirst_ref, h_ref, wd_ref, bd_ref,
                        o_ref, wdb_ref):
    w = pl.program_id(0)

    @pl.when(ifirst_ref[w] == 1)
    def _():
        wdb_ref[...] = wd_ref[0].astype(BF16)

    @pl.when(ivalid_ref[w] == 1)
    def _():
        out = _dot(h_ref[...], wdb_ref[...]) + bd_ref[0]
        half = out.shape[1] // 2
        _store_row_tiles(o_ref, _pack_rows(out[:, :half], out[:, half:]))

    @pl.when(ivalid_ref[w] == 0)
    def _():
        o_ref[...] = jnp.zeros(o_ref.shape, o_ref.dtype)


def _work_items(tiles_e, n_tiles_max, n_col):
    ne = tiles_e.shape[0]
    cum = jnp.cumsum(tiles_e)
    total_tiles = cum[-1]
    first_tile = cum - tiles_e
    item_end = cum * n_col
    w = jnp.arange(n_tiles_max * n_col, dtype=jnp.int32)
    e = jnp.minimum(jnp.sum(w[:, None] >= item_end[None, :], axis=1), ne - 1).astype(jnp.int32)
    te = jnp.maximum(tiles_e[e], 1)
    local = w - (item_end[e] - tiles_e[e] * n_col)
    valid = w < total_tiles * n_col
    spare = jnp.maximum(n_tiles_max - total_tiles, 1)
    j = w - total_tiles * n_col
    col = jnp.where(valid, local // te, j // spare)
    rt = jnp.where(valid, first_tile[e] + local % te, total_tiles + j % spare)
    first = jnp.where(valid, (local % te) == 0, False)
    i32 = lambda a: a.astype(jnp.int32)
    return i32(e), i32(col), i32(rt), i32(valid), i32(first)


def _expert_up(items, xs, w_gate, w_up, b_gate, b_up, tm, tn):
    ne, d, f = w_gate.shape
    rh = d // 2 // LANES
    n_rows = xs.shape[0] // rh
    tn = min(tn, f)
    n_items = items[0].shape[0]
    wspec = pl.BlockSpec((1, d, tn), lambda w, ie, ic, irt, iv, ifi: (ie[w], 0, ic[w]))
    bspec = pl.BlockSpec((1, 1, tn), lambda w, ie, ic, irt, iv, ifi: (ie[w], 0, ic[w]))
    grid_spec = pltpu.PrefetchScalarGridSpec(
        num_scalar_prefetch=5,
        grid=(n_items,),
        in_specs=[pl.BlockSpec((tm * rh, LANES), lambda w, ie, ic, irt, iv, ifi: (irt[w], 0)),
                  wspec, wspec, bspec, bspec],
        out_specs=pl.BlockSpec((tm, tn), lambda w, ie, ic, irt, iv, ifi: (irt[w], ic[w])),
        scratch_shapes=[pltpu.VMEM((d, tn), BF16), pltpu.VMEM((d, tn), BF16)],
    )
    return pl.pallas_call(
        _expert_up_kernel,
        grid_spec=grid_spec,
        out_shape=jax.ShapeDtypeStruct((n_rows, f), BF16),
        compiler_params=_cparams(("arbitrary",)),
    )(*items, xs, w_gate, w_up, b_gate.reshape(ne, 1, f), b_up.reshape(ne, 1, f))


def _expert_down(items, h, w_down, b_down, tm):
    n_rows, f = h.shape
    ne, _, d = w_down.shape
    rh = d // 2 // LANES
    n_items = items[0].shape[0]
    grid_spec = pltpu.PrefetchScalarGridSpec(
        num_scalar_prefetch=5,
        grid=(n_items,),
        in_specs=[pl.BlockSpec((tm, f), lambda w, ie, ic, irt, iv, ifi: (irt[w], 0)),
                  pl.BlockSpec((1, f, d), lambda w, ie, ic, irt, iv, ifi: (ie[w], 0, 0)),
                  pl.BlockSpec((1, 1, d), lambda w, ie, ic, irt, iv, ifi: (ie[w], 0, 0))],
        out_specs=pl.BlockSpec((tm * rh, LANES), lambda w, ie, ic, irt, iv, ifi: (irt[w], 0)),
        scratch_shapes=[pltpu.VMEM((f, d), BF16)],
    )
    return pl.pallas_call(
        _expert_down_kernel,
        grid_spec=grid_spec,
        out_shape=jax.ShapeDtypeStruct((n_rows * rh, LANES), jnp.uint32),
        compiler_params=_cparams(("arbitrary",)),
    )(*items, h, w_down, b_down.reshape(ne, 1, d))


def _combine_kernel(dest_ref, os_ref, tg_ref, x2_ref, g_ref, b_ref, o_ref, buf_ref, sem,
                    *, rows, alpha, rh, n_steps):
    i = pl.program_id(0)
    slot = i % 2

    def gather(step, into):
        base = step * rows * TOP_K

        def issue(r, carry):
            dst = pl.multiple_of(r * rh, rh)
            for k in range(TOP_K):
                src = pl.multiple_of(dest_ref[base + r * TOP_K + k] * rh, rh)
                pltpu.make_async_copy(os_ref.at[pl.ds(src, rh)], buf_ref.at[into, k, pl.ds(dst, rh)],
                                      sem.at[into]).start(priority=k % 2)
            return carry

        lax.fori_loop(0, rows, issue, 0, unroll=ISSUE_UNROLL)

    @pl.when(i == 0)
    def _():
        gather(0, 0)

    @pl.when(i + 1 < n_steps)
    def _():
        gather(i + 1, 1 - slot)

    for k in range(TOP_K):
        pltpu.make_async_copy(os_ref.at[pl.ds(0, rows * rh)], buf_ref.at[slot, k],
                              sem.at[slot]).wait()

    tg = tg_ref[...]
    lo_sum = hi_sum = None
    for k in range(TOP_K):
        lo, hi = _unpack_rows(_load_row_tiles(buf_ref.at[slot, k], rh))
        gate = tg[:, k:k + 1]
        lo_sum = gate * lo if k == 0 else lo_sum + gate * lo
        hi_sum = gate * hi if k == 0 else hi_sum + gate * hi
    y = jnp.concatenate([lo_sum, hi_sum], axis=1)
    o_ref[...] = _layer_norm(alpha * x2_ref[...] + y, g_ref[...], b_ref[...])


def _combine(dest, out_sorted, tg, x2, ln_g, ln_b, alpha, rows=256):
    t, d = x2.shape
    rows = min(rows, t)
    rh = d // 2 // LANES
    row = lambda a: a.reshape(1, -1).astype(F32)
    grid_spec = pltpu.PrefetchScalarGridSpec(
        num_scalar_prefetch=1,
        grid=(t // rows,),
        in_specs=[pl.BlockSpec(memory_space=pl.ANY),
                  pl.BlockSpec((rows, LANES), lambda i, s: (i, 0)),
                  pl.BlockSpec((rows, d), lambda i, s: (i, 0)),
                  pl.BlockSpec((1, d), lambda i, s: (0, 0)),
                  pl.BlockSpec((1, d), lambda i, s: (0, 0))],
        out_specs=pl.BlockSpec((rows, d), lambda i, s: (i, 0)),
        scratch_shapes=[pltpu.VMEM((2, TOP_K, rows * rh, LANES), jnp.uint32),
                        pltpu.SemaphoreType.DMA((2,))],
    )
    return pl.pallas_call(
        functools.partial(_combine_kernel, rows=rows, alpha=alpha, rh=rh, n_steps=t // rows),
        grid_spec=grid_spec,
        out_shape=jax.ShapeDtypeStruct((t, d), F32),
        compiler_params=_cparams(("arbitrary",)),
    )(dest, out_sorted, tg, x2, row(ln_g), row(ln_b))


def _moe(x2, x2p, te, tg, rk, counts, w_gate, b_gate, w_up, b_up, w_down, b_down, ln_g, ln_b,
         alpha, tm=512, tn_up=1024):
    t, d = x2.shape
    ne = w_gate.shape[0]
    n_assign = t * TOP_K
    tm = min(tm, t)
    n_tiles_max = -(-n_assign // tm) + ne
    n_rows = n_tiles_max * tm

    counts = counts[0, :ne].astype(jnp.int32)
    tiles_e = (counts + tm - 1) // tm
    pad_start = (jnp.cumsum(tiles_e) - tiles_e) * tm
    hit = te[:, :TOP_K, None] == jnp.arange(ne, dtype=jnp.int32)
    dest = (jnp.sum(jnp.where(hit, pad_start, 0), axis=-1) + rk[:, :TOP_K]).reshape(n_assign)

    cum_tiles = jnp.cumsum(tiles_e)
    last_tile = jnp.where(tiles_e > 0, cum_tiles - 1, -1)
    tail = cum_tiles[-1] + jnp.arange(ne, dtype=jnp.int32)
    fill_tiles = jnp.concatenate([last_tile, jnp.where(tail < n_tiles_max, tail, -1)])

    xs = _dispatch(dest, fill_tiles.astype(jnp.int32), x2p, n_rows, tm, d // 2 // LANES)
    f = w_gate.shape[2]
    tn_up = min(tn_up, f)
    items_up = _work_items(tiles_e, n_tiles_max, f // tn_up)
    h = _expert_up(items_up, xs, w_gate, w_up, b_gate, b_up, tm, tn_up)
    items_dn = _work_items(tiles_e, n_tiles_max, 1)
    out_sorted = _expert_down(items_dn, h, w_down, b_down, tm)
    return _combine(dest, out_sorted, tg, x2, ln_g, ln_b, alpha)


def kernel(x, mem, w_in, conv_w, attn_subln_w, lambda_q1, lambda_k1, lambda_q2, lambda_k2, w_out, ln1_g, ln1_b, mem_wq, mem_wkv, mem_wo, ln2_g, ln2_b, router_w, router_b, w_gate, b_gate, w_up, b_up, w_down, b_down, ln3_g, ln3_b):
    bsz, seq, d = x.shape
    n_mem = mem.shape[1]
    depth = w_in.shape[0]
    alpha = (2.0 * depth) ** 0.25
    slopes = jnp.exp2(-8.0 / N_ATTN_HEADS * jnp.arange(1, N_ATTN_HEADS + 1, dtype=F32))
    xf = x.reshape(bsz * seq, d)
    memf = mem.reshape(bsz * n_mem, d)
    for l in range(depth):
        lambda_init = 0.8 - 0.6 * math.exp(-0.3 * l)
        proj = _projection(xf, w_in[l].astype(BF16), 1024, 1024)
        oattn = _diff_attention(proj.reshape(bsz, seq, -1), slopes, lambda_q1[l], lambda_k1[l],
                                lambda_q2[l], lambda_k2[l], attn_subln_w[l], lambda_init)
        x1, x1b = _out_projection(oattn.reshape(bsz * seq, -1), proj, xf, conv_w[l], w_out[l],
                                  ln1_g[l], ln1_b[l], seq, alpha)
        kv = _projection(memf, mem_wkv[l].astype(BF16), 1024, 1024)
        x2, x2p, te, tg, rk, counts = _memory_attention(
            x1b, x1, kv, mem_wq[l], mem_wo[l], ln2_g[l], ln2_b[l], router_w[l], router_b[l],
            seq, n_mem, alpha)
        xf = _moe(x2, x2p, te, tg, rk, counts, w_gate[l], b_gate[l], w_up[l], b_up[l],
                  w_down[l], b_down[l], ln3_g[l], ln3_b[l], alpha)
    return xf.reshape(bsz, seq, d)
```

```python
import functools
import math

import jax
import jax.numpy as jnp
from jax import lax
from jax.experimental import pallas as pl
from jax.experimental.pallas import tpu as pltpu

N_ATTN_HEADS = 8
DIFF_HEAD_DIM = 64
HEAD_WIDTH = 2 * DIFF_HEAD_DIM
MEM_HEADS = 4
TOP_K = 4
CONV_TAPS = 3
SWIGLU_LIMIT = 7.0
SWIGLU_ALPHA = 1.702
LN_EPS = 1e-5

LANES = 128
SUBLANES = 8
VMEM_LIMIT = 56 * 1024 * 1024
NEG_BIG = -1e30
ALIBI_SPLIT = 16
ALIBI_OFFSET_LANE = 4
QUERY_BLOCK = 256
ONES_ROWS = 16
UNDERFLOW_NATS = 104.0
NORM_SLACK = 1.01
ISSUE_UNROLL = 8

F32 = jnp.float32
BF16 = jnp.bfloat16


def _cparams(sem, vmem=VMEM_LIMIT):
    return pltpu.CompilerParams(dimension_semantics=sem, vmem_limit_bytes=vmem)


def _resident(shape, index_map):
    return pl.BlockSpec(shape, index_map, pipeline_mode=pl.Buffered(1))


def _layer_norm(z, g, b):
    mu = jnp.mean(z, axis=-1, keepdims=True)
    d = z - mu
    var = jnp.mean(d * d, axis=-1, keepdims=True)
    return d * lax.rsqrt(var + LN_EPS) * g + b


def _dot(a, b):
    return jnp.dot(a, b, preferred_element_type=F32)


def _proj_kernel(x_ref, w_ref, o_ref, xb_ref):
    @pl.when(pl.program_id(1) == 0)
    def _():
        xb_ref[...] = x_ref[...].astype(BF16)

    o_ref[...] = _dot(xb_ref[...], w_ref[...]).astype(o_ref.dtype)


def _projection(x, w, tm, tn):
    m, k = x.shape
    n = w.shape[1]
    tm, tn = min(tm, m), min(tn, n)
    return pl.pallas_call(
        _proj_kernel,
        grid=(m // tm, n // tn),
        in_specs=[pl.BlockSpec((tm, k), lambda i, j: (i, 0)),
                  pl.BlockSpec((k, tn), lambda i, j: (0, j))],
        out_specs=pl.BlockSpec((tm, tn), lambda i, j: (i, j)),
        out_shape=jax.ShapeDtypeStruct((m, n), BF16),
        scratch_shapes=[pltpu.VMEM((tm, k), BF16)],
        compiler_params=_cparams(("parallel", "arbitrary")),
    )(x, w)


def _attn_kernel(slopes_ref, q_ref, k_ref, v_ref, qft_ref, kf_ref, bdt_ref, lq1_ref, lk1_ref, lq2_ref,
                 lk2_ref, sub_ref, o_ref, lhst_ref, vt_ref, st_ref, pt_ref, m_ref, mx_ref, alpha_ref,
                 acct_ref, kn_ref, *, tq, tk, seq, lambda_init):
    h = pl.program_id(1)
    qi = pl.program_id(2)
    slope = slopes_ref[h]

    @pl.when(qi == 0)
    def _():
        half_lane = lax.broadcasted_iota(jnp.int32, (tk, HEAD_WIDTH), 1) < DIFF_HEAD_DIM
        kn1 = jnp.zeros((1, 1), F32)
        kn2 = jnp.zeros((1, 1), F32)
        for c in range(seq // tk):
            vt_ref[c, :HEAD_WIDTH, :] = v_ref[0, c * tk:(c + 1) * tk, :].astype(F32).T.astype(BF16)
            vt_ref[c, HEAD_WIDTH:, :] = jnp.ones((ONES_ROWS, tk), BF16)
            kc = k_ref[0, c * tk:(c + 1) * tk, :].astype(F32)
            sq = kc * kc
            n1 = jnp.sum(jnp.where(half_lane, sq, 0.0), axis=1, keepdims=True)
            n2 = jnp.sum(jnp.where(half_lane, 0.0, sq), axis=1, keepdims=True)
            kn1 = jnp.maximum(kn1, jnp.max(n1, axis=0, keepdims=True))
            kn2 = jnp.maximum(kn2, jnp.max(n2, axis=0, keepdims=True))
        kn_ref[...] = jnp.where(lax.broadcasted_iota(jnp.int32, kn_ref.shape, 1) == 0, kn1, kn2)

    qt = (q_ref[0].astype(F32) * (DIFF_HEAD_DIM ** -0.5)).T
    row = lax.broadcasted_iota(jnp.int32, qt.shape, 0)

    qsq = qt * qt
    qn1 = jnp.max(jnp.sum(jnp.where(row < DIFF_HEAD_DIM, qsq, 0.0), axis=0, keepdims=True),
                  axis=1, keepdims=True)
    qn2 = jnp.max(jnp.sum(jnp.where(row >= DIFF_HEAD_DIM, qsq, 0.0), axis=0, keepdims=True),
                  axis=1, keepdims=True)
    kn = kn_ref[...]
    qk = jnp.sqrt(jnp.maximum(qn1 * kn[:, 0:1], qn2 * kn[:, 1:2]))
    reach = jnp.minimum((UNDERFLOW_NATS + 2.0 * NORM_SLACK * qk) / slope, 4.0 * seq)
    q_lo = (qi * tq).astype(F32)
    lo_f = jnp.floor((q_lo - (tk - 1) - reach) / tk) + 1.0
    hi_f = jnp.ceil((q_lo + (tq - 1) + reach) / tk)
    n_tiles = seq // tk
    per_q = tq // tk
    lo = jnp.clip(lo_f.astype(jnp.int32)[0, 0], 0, qi * per_q)
    hi = jnp.clip(hi_f.astype(jnp.int32)[0, 0], (qi + 1) * per_q, n_tiles)
    if n_tiles > 1:
        short = hi - lo < 2
        lo, hi = (jnp.where(short & (lo > 0), lo - 1, lo), jnp.where(short & (lo == 0), hi + 1, hi))
    qqt = jnp.concatenate([jnp.where(row < DIFF_HEAD_DIM, qt, 0.0),
                           jnp.where(row >= DIFF_HEAD_DIM, qt, 0.0)], axis=1).astype(BF16)
    fft = jnp.concatenate([qft_ref[0], qft_ref[0]], axis=1)
    lhst_ref[0] = jnp.concatenate([qqt, fft], axis=0)
    lhst_ref[1] = jnp.concatenate([qqt, -fft], axis=0)
    lhst_ref[2] = jnp.concatenate([qqt, jnp.zeros_like(fft)], axis=0)

    m_ref[...] = jnp.full(m_ref.shape, NEG_BIG, F32)
    acct_ref[...] = jnp.zeros(acct_ref.shape, F32)
    kf = kf_ref[0]
    klane = lax.broadcasted_iota(jnp.int32, (tk, HEAD_WIDTH), 1)

    blocks = [slice(qb * QUERY_BLOCK, (qb + 1) * QUERY_BLOCK)
              for qb in range(2 * tq // QUERY_BLOCK)]

    def scores(kt):
        start = pl.multiple_of(kt * tk, tk)
        overlap = kt - qi * per_q
        variant = jnp.where(overlap < 0, 0, jnp.where(overlap >= per_q, 1, 2))
        c = slope * (kt * tk - qi * tq).astype(F32)
        feat = jnp.where(klane == ALIBI_OFFSET_LANE, jnp.full(kf.shape, c, F32).astype(BF16), kf)
        k_aug = jnp.concatenate([k_ref[0, pl.ds(start, tk), :], feat], axis=1)
        for cols in blocks:
            st = _dot(k_aug, lhst_ref[variant, :, cols])
            st_ref[:, cols] = st
            mx_ref[:, cols] = jnp.max(st, axis=0, keepdims=True)

        @pl.when(variant == 2)
        def _():
            bias_t = bdt_ref[0, jnp.clip(overlap, 0, per_q - 1)]
            for cols in blocks:
                di = cols.start % tq
                st = st_ref[:, cols] + bias_t[:, di:di + QUERY_BLOCK]
                st_ref[:, cols] = st
                mx_ref[:, cols] = jnp.max(st, axis=0, keepdims=True)

    def probs():
        for cols in blocks:
            st = st_ref[:, cols]
            m_prev = m_ref[:, cols]
            m_new = jnp.maximum(m_prev, mx_ref[:, cols])
            alpha_ref[:, cols] = jnp.exp(m_prev - m_new)
            pt_ref[:, cols] = jnp.exp(st - m_new).astype(BF16)
            m_ref[:, cols] = m_new

    def accumulate(kt):
        vt = vt_ref[kt]
        for cols in blocks:
            acct_ref[:, cols] = alpha_ref[:, cols] * acct_ref[:, cols] + _dot(vt, pt_ref[:, cols])

    def step(j, carry):
        accumulate(j - 2)
        probs()
        scores(j)
        return carry

    scores(lo)
    if n_tiles > 1:
        probs()
        scores(lo + 1)
        lax.fori_loop(lo + 2, hi, step, 0)
        accumulate(hi - 2)
    probs()
    accumulate(hi - 1)

    lam = (jnp.exp(jnp.sum(lq1_ref[...] * lk1_ref[...], keepdims=True))
           - jnp.exp(jnp.sum(lq2_ref[...] * lk2_ref[...], keepdims=True)) + lambda_init)
    acct = acct_ref[...]
    ot = acct[:HEAD_WIDTH, :] / acct[HEAD_WIDTH:HEAD_WIDTH + 1, :]
    ot = ot[:, :tq] - lam * ot[:, tq:]
    ms = jnp.mean(ot * ot, axis=0, keepdims=True)
    yt = ot * lax.rsqrt(ms + LN_EPS) * sub_ref[...] * (1.0 - lambda_init)
    o_ref[0] = yt.T.astype(o_ref.dtype)


def _alibi_features(slopes, t):
    pos = jnp.arange(t, dtype=jnp.int32)
    hi = (pos // ALIBI_SPLIT * ALIBI_SPLIT).astype(F32)[None, :]
    lo = (pos % ALIBI_SPLIT).astype(F32)[None, :]
    sl = slopes[:, None]
    one = jnp.ones((slopes.shape[0], t), F32)
    zero = jnp.zeros_like(one)
    pad = [zero] * (HEAD_WIDTH - 5)
    qf = jnp.stack([one, one, -sl * hi, -sl * lo, one] + pad, axis=-1)
    kf = jnp.stack([sl * hi, sl * lo, one, one, zero] + pad, axis=-1)
    assert ALIBI_OFFSET_LANE == 4
    return qf.astype(BF16), kf.astype(BF16)


def _diff_attention(proj3, slopes, lq1, lk1, lq2, lk2, subln, lambda_init, tq=1024, tk=512):
    bsz, seq, _ = proj3.shape
    nh = N_ATTN_HEADS
    tq, tk = min(tq, seq), min(tk, seq)
    assert tq % tk == 0 and seq % tq == 0 and (2 * tq) % QUERY_BLOCK == 0
    qft = _alibi_features(slopes, tq)[0].transpose(0, 2, 1)
    kf = _alibi_features(slopes, tk)[1]
    di = jnp.arange(tq, dtype=jnp.int32)[None, None, :]
    dj = jnp.arange(tk, dtype=jnp.int32)[None, :, None]
    r = jnp.arange(tq // tk, dtype=jnp.int32)[:, None, None]
    bias_t = -slopes[:, None, None, None] * jnp.abs(di - dj - r * tk).astype(F32)[None]
    vec = lambda a: a.reshape(1, -1).astype(F32)
    small = lambda n: pl.BlockSpec((1, n), lambda b, h, i, s: (0, 0))
    grid_spec = pltpu.PrefetchScalarGridSpec(
        num_scalar_prefetch=1,
        grid=(bsz, nh, seq // tq),
        in_specs=[
            pl.BlockSpec((1, tq, HEAD_WIDTH), lambda b, h, i, s: (b, i, h)),
            pl.BlockSpec((1, seq, HEAD_WIDTH), lambda b, h, i, s: (b, 0, nh + h)),
            pl.BlockSpec((1, seq, HEAD_WIDTH), lambda b, h, i, s: (b, 0, 2 * nh + h)),
            pl.BlockSpec((1, HEAD_WIDTH, tq), lambda b, h, i, s: (h, 0, 0)),
            pl.BlockSpec((1, tk, HEAD_WIDTH), lambda b, h, i, s: (h, 0, 0)),
            pl.BlockSpec((1, tq // tk, tk, tq), lambda b, h, i, s: (h, 0, 0, 0)),
            small(DIFF_HEAD_DIM), small(DIFF_HEAD_DIM), small(DIFF_HEAD_DIM), small(DIFF_HEAD_DIM),
            pl.BlockSpec((HEAD_WIDTH, 1), lambda b, h, i, s: (0, 0)),
        ],
        out_specs=pl.BlockSpec((1, tq, HEAD_WIDTH), lambda b, h, i, s: (b, i, h)),
        scratch_shapes=[pltpu.VMEM((3, 2 * HEAD_WIDTH, 2 * tq), BF16),
                        pltpu.VMEM((seq // tk, HEAD_WIDTH + ONES_ROWS, tk), BF16),
                        pltpu.VMEM((tk, 2 * tq), F32),
                        pltpu.VMEM((tk, 2 * tq), BF16),
                        pltpu.VMEM((1, 2 * tq), F32),
                        pltpu.VMEM((1, 2 * tq), F32),
                        pltpu.VMEM((1, 2 * tq), F32),
                        pltpu.VMEM((HEAD_WIDTH + ONES_ROWS, 2 * tq), F32),
                        pltpu.VMEM((1, LANES), F32)],
    )
    return pl.pallas_call(
        functools.partial(_attn_kernel, tq=tq, tk=tk, seq=seq, lambda_init=lambda_init),
        grid_spec=grid_spec,
        out_shape=jax.ShapeDtypeStruct((bsz, seq, nh * HEAD_WIDTH), BF16),
        compiler_params=_cparams(("parallel", "parallel", "arbitrary")),
    )(slopes, proj3, proj3, proj3, qft, kf, bias_t,
      vec(lq1), vec(lk1), vec(lq2), vec(lk2), subln.reshape(-1, 1).astype(F32))


def _outproj_kernel(oa_ref, gb_ref, gc_ref, u_ref, cp_ref, up_ref, cn_ref, un_ref, cw_ref,
                    wa_ref, wc_ref, x_ref, g_ref, b_ref, x1_ref, x1b_ref, *, tm, seq, alpha):
    i = pl.program_id(0)
    cu = gc_ref[...].astype(F32) * u_ref[...].astype(F32)
    row = lax.broadcasted_iota(jnp.int32, cu.shape, 0)
    last = SUBLANES - 1
    prev_edge = cp_ref[last:last + 1, :].astype(F32) * up_ref[last:last + 1, :].astype(F32)
    next_edge = cn_ref[0:1, :].astype(F32) * un_ref[0:1, :].astype(F32)
    prev_edge = jnp.where((i * tm) % seq == 0, 0.0, prev_edge)
    next_edge = jnp.where(((i + 1) * tm) % seq == 0, 0.0, next_edge)
    cu_prev = jnp.where(row == 0, prev_edge, pltpu.roll(cu, 1, 0))
    cu_next = jnp.where(row == tm - 1, next_edge, pltpu.roll(cu, tm - 1, 0))
    cw = cw_ref[...]
    conv = cu_prev * cw[0:1, :] + cu * cw[1:2, :] + cu_next * cw[2:3, :]
    oc = (gb_ref[...].astype(F32) * conv).astype(BF16)
    y = _dot(oa_ref[...], wa_ref[...]) + _dot(oc, wc_ref[...])
    x1 = _layer_norm(alpha * x_ref[...] + y, g_ref[...], b_ref[...])
    x1_ref[...] = x1
    x1b_ref[...] = x1.astype(BF16)


def _out_projection(oattn, proj, x, conv_w, w_out, ln_g, ln_b, seq, alpha, tm=512):
    t, d = x.shape
    aw = oattn.shape[1]
    cwid = d - aw
    assert aw == cwid and proj.shape[1] == 3 * aw + 3 * cwid
    tm = min(tm, seq)
    hb = tm // SUBLANES
    nhb = t // SUBLANES
    cb = 3 * aw // cwid
    prev_map = lambda c: (lambda i: (jnp.maximum(i * hb - 1, 0), c))
    next_map = lambda c: (lambda i: (jnp.minimum((i + 1) * hb, nhb - 1), c))
    wa = w_out[:aw].astype(BF16)
    wc = w_out[aw:].astype(BF16)
    row = lambda a: a.reshape(1, -1).astype(F32)
    return pl.pallas_call(
        functools.partial(_outproj_kernel, tm=tm, seq=seq, alpha=alpha),
        grid=(t // tm,),
        in_specs=[
            pl.BlockSpec((tm, aw), lambda i: (i, 0)),
            pl.BlockSpec((tm, cwid), lambda i: (i, cb)),
            pl.BlockSpec((tm, cwid), lambda i: (i, cb + 1)),
            pl.BlockSpec((tm, cwid), lambda i: (i, cb + 2)),
            pl.BlockSpec((SUBLANES, cwid), prev_map(cb + 1)),
            pl.BlockSpec((SUBLANES, cwid), prev_map(cb + 2)),
            pl.BlockSpec((SUBLANES, cwid), next_map(cb + 1)),
            pl.BlockSpec((SUBLANES, cwid), next_map(cb + 2)),
            _resident((CONV_TAPS, cwid), lambda i: (0, 0)),
            _resident((aw, d), lambda i: (0, 0)),
            _resident((cwid, d), lambda i: (0, 0)),
            pl.BlockSpec((tm, d), lambda i: (i, 0)),
            _resident((1, d), lambda i: (0, 0)),
            _resident((1, d), lambda i: (0, 0)),
        ],
        out_specs=[pl.BlockSpec((tm, d), lambda i: (i, 0)),
                   pl.BlockSpec((tm, d), lambda i: (i, 0))],
        out_shape=[jax.ShapeDtypeStruct((t, d), F32), jax.ShapeDtypeStruct((t, d), BF16)],
        compiler_params=_cparams(("parallel",)),
    )(oattn, proj, proj, proj, proj, proj, proj, proj, conv_w.astype(F32), wa, wc, x,
      row(ln_g), row(ln_b))


def _memattn_kernel(x1b_ref, x1_ref, wq_ref, kt_ref, v_ref, wo_ref, g_ref, b_ref, rw_ref, rb_ref,
                    x2_ref, x2p_ref, te_ref, tg_ref, rk_ref, cnt_ref, carry_ref, *, tm, alpha):
    i = pl.program_id(0)
    d = x1_ref.shape[1]
    hd = d // MEM_HEADS

    @pl.when(i == 0)
    def _():
        carry_ref[...] = jnp.zeros(carry_ref.shape, F32)

    q = _dot(x1b_ref[...], wq_ref[...]).astype(BF16)
    scale = hd ** -0.5
    heads = []
    for hh in range(MEM_HEADS):
        cols = slice(hh * hd, (hh + 1) * hd)
        s = _dot(q[:, cols], kt_ref[0, cols, :]) * scale
        p = jnp.exp(s - jnp.max(s, axis=1, keepdims=True))
        p = p / jnp.sum(p, axis=1, keepdims=True)
        heads.append(_dot(p.astype(BF16), v_ref[:, cols]).astype(BF16))
    xa = _dot(jnp.concatenate(heads, axis=1), wo_ref[...])
    x2 = _layer_norm(alpha * x1_ref[...] + xa, g_ref[...], b_ref[...])
    x2_ref[...] = x2

    _store_row_tiles(x2p_ref, _pack_rows(x2[:, :d // 2], x2[:, d // 2:]))

    xh = x2.astype(BF16)
    xl = (x2 - xh.astype(F32)).astype(BF16)
    both = _dot(xh, rw_ref[...])
    logits = both[:, :LANES] + both[:, LANES:] + _dot(xl, rw_ref[:, :LANES]) + rb_ref[...]

    lane = lax.broadcasted_iota(jnp.int32, logits.shape, 1)
    work = logits
    vals, sels, idxs = [], [], []
    for _ in range(TOP_K):
        mk = jnp.max(work, axis=1, keepdims=True)
        idx = jnp.min(jnp.where(work == mk, lane, LANES), axis=1, keepdims=True)
        sel = lane == idx
        vals.append(mk)
        idxs.append(idx)
        sels.append(sel)
        work = jnp.where(sel, -jnp.inf, work)
    exps = [jnp.exp(v - vals[0]) for v in vals]
    denom = exps[0]
    for e in exps[1:]:
        denom = denom + e

    onehot = jnp.zeros(logits.shape, F32)
    for sel in sels:
        onehot = onehot + sel.astype(F32)
    lower = (lax.broadcasted_iota(jnp.int32, (tm, tm), 0)
             > lax.broadcasted_iota(jnp.int32, (tm, tm), 1)).astype(BF16)
    before = _dot(lower, onehot.astype(BF16)) + carry_ref[...]

    te = jnp.zeros(logits.shape, jnp.int32)
    tg = jnp.zeros(logits.shape, F32)
    rk = jnp.zeros(logits.shape, jnp.int32)
    for k in range(TOP_K):
        rank_k = jnp.sum(jnp.where(sels[k], before, 0.0), axis=1, keepdims=True)
        te = jnp.where(lane == k, idxs[k], te)
        tg = jnp.where(lane == k, exps[k] / denom, tg)
        rk = jnp.where(lane == k, rank_k.astype(jnp.int32), rk)
    te_ref[...] = te
    tg_ref[...] = tg
    rk_ref[...] = rk
    carry_ref[...] = carry_ref[...] + jnp.sum(onehot, axis=0, keepdims=True)
    cnt_ref[...] = carry_ref[...]


def _memory_attention(x1b, x1, kv, wq, wo, ln_g, ln_b, router_w, router_b, seq, n_mem, alpha, tm=256):
    t, d = x1.shape
    tm = min(tm, seq)
    ne = router_w.shape[1]
    assert ne <= LANES
    rw = jnp.zeros((d, LANES), F32).at[:, :ne].set(router_w.astype(F32))
    rwh = rw.astype(BF16)
    rwl = (rw - rwh.astype(F32)).astype(BF16)
    rb = jnp.full((1, LANES), NEG_BIG, F32).at[0, :ne].set(router_b.astype(F32))
    row = lambda a: a.reshape(1, -1).astype(F32)
    kt = kv[:, :d].reshape(t // seq, n_mem, d).transpose(0, 2, 1)
    tiles_per_seq = seq // tm
    tile_out = lambda w, dt: (pl.BlockSpec((tm, w), lambda i: (i, 0)), jax.ShapeDtypeStruct((t, w), dt))
    rh = d // 2 // LANES
    x2p_out = (pl.BlockSpec((tm * rh, LANES), lambda i: (i, 0)),
               jax.ShapeDtypeStruct((t * rh, LANES), jnp.uint32))
    outs = [tile_out(d, F32), x2p_out, tile_out(LANES, jnp.int32),
            tile_out(LANES, F32), tile_out(LANES, jnp.int32),
            (pl.BlockSpec((1, LANES), lambda i: (0, 0)), jax.ShapeDtypeStruct((1, LANES), F32))]
    return pl.pallas_call(
        functools.partial(_memattn_kernel, tm=tm, alpha=alpha),
        grid=(t // tm,),
        in_specs=[
            pl.BlockSpec((tm, d), lambda i: (i, 0)),
            pl.BlockSpec((tm, d), lambda i: (i, 0)),
            _resident((d, d), lambda i: (0, 0)),
            pl.BlockSpec((1, d, n_mem), lambda i: (i // tiles_per_seq, 0, 0)),
            pl.BlockSpec((n_mem, d), lambda i: (i // tiles_per_seq, 1)),
            _resident((d, d), lambda i: (0, 0)),
            _resident((1, d), lambda i: (0, 0)),
            _resident((1, d), lambda i: (0, 0)),
            _resident((d, 2 * LANES), lambda i: (0, 0)),
            _resident((1, LANES), lambda i: (0, 0)),
        ],
        out_specs=[o[0] for o in outs],
        out_shape=[o[1] for o in outs],
        scratch_shapes=[pltpu.VMEM((1, LANES), F32)],
        compiler_params=_cparams(("arbitrary",)),
    )(x1b, x1, wq.astype(BF16), kt, kv, wo.astype(BF16), row(ln_g), row(ln_b),
      jnp.concatenate([rwh, rwl], axis=1), rb)


def _dispatch_kernel(dest_ref, fill_ref, x2p_ref, xs_ref, zero_ref, stage_ref, sem, zsem,
                     *, rows, tm, n_fill, rh, n_steps):
    i = pl.program_id(0)

    @pl.when(i == 0)
    def _():
        zero_ref[...] = jnp.zeros(zero_ref.shape, zero_ref.dtype)

        def fill(n):
            start = pl.multiple_of(jnp.maximum(fill_ref[n], 0) * (tm * rh), tm * rh)
            return pltpu.make_async_copy(zero_ref, xs_ref.at[pl.ds(start, tm * rh)], zsem)

        for n in range(n_fill):
            @pl.when(fill_ref[n] >= 0)
            def _(n=n):
                fill(n).start()
        for n in range(n_fill):
            @pl.when(fill_ref[n] >= 0)
            def _(n=n):
                fill(n).wait()

    base = i * rows * TOP_K
    slot = i % 2
    stage_ref[slot] = x2p_ref[...]

    def issue(r, carry):
        src = pl.multiple_of(r * rh, rh)
        for k in range(TOP_K):
            dst = pl.multiple_of(dest_ref[base + r * TOP_K + k] * rh, rh)
            pltpu.make_async_copy(stage_ref.at[slot, pl.ds(src, rh)], xs_ref.at[pl.ds(dst, rh)],
                                  sem.at[slot]).start(priority=k % 2)
        return carry

    lax.fori_loop(0, rows, issue, 0, unroll=ISSUE_UNROLL)

    def drain(s):
        for _ in range(TOP_K):
            pltpu.make_async_copy(x2p_ref, xs_ref.at[pl.ds(0, rows * rh)], sem.at[s]).wait()

    @pl.when(i > 0)
    def _():
        drain(1 - slot)

    @pl.when(i == n_steps - 1)
    def _():
        drain(slot)


def _dispatch(dest, fill_tiles, x2p, n_rows, tm, rh, rows=512):
    t = x2p.shape[0] // rh
    rows = min(rows, t)
    n_fill = fill_tiles.shape[0]
    grid_spec = pltpu.PrefetchScalarGridSpec(
        num_scalar_prefetch=2,
        grid=(t // rows,),
        in_specs=[pl.BlockSpec((rows * rh, LANES), lambda i, d, z: (i, 0))],
        out_specs=pl.BlockSpec(memory_space=pl.ANY),
        scratch_shapes=[pltpu.VMEM((tm * rh, LANES), x2p.dtype),
                        pltpu.VMEM((2, rows * rh, LANES), x2p.dtype),
                        pltpu.SemaphoreType.DMA((2,)), pltpu.SemaphoreType.DMA],
    )
    return pl.pallas_call(
        functools.partial(_dispatch_kernel, rows=rows, tm=tm, n_fill=n_fill, rh=rh,
                          n_steps=t // rows),
        grid_spec=grid_spec,
        out_shape=jax.ShapeDtypeStruct((n_rows * rh, LANES), x2p.dtype),
        compiler_params=_cparams(("arbitrary",)),
    )(dest, fill_tiles, x2p)


def _pack_rows(lo, hi):
    lo = lax.bitcast_convert_type(lo.astype(BF16).astype(F32), jnp.uint32)
    hi = lax.bitcast_convert_type(hi.astype(BF16).astype(F32), jnp.uint32)
    return (lo >> 16) | (hi & jnp.uint32(0xFFFF0000))


def _store_row_tiles(ref, words):
    rows, width = words.shape
    chunks = width // LANES
    for c in range(chunks):
        ref[pl.ds(c, rows, stride=chunks), :] = words[:, c * LANES:(c + 1) * LANES]


def _load_row_tiles(ref, chunks):
    rows = ref.shape[0] // chunks
    return jnp.concatenate([ref[pl.ds(c, rows, stride=chunks), :] for c in range(chunks)], axis=1)


def _unpack_rows(words):
    lo = lax.bitcast_convert_type(words << 16, F32)
    hi = lax.bitcast_convert_type(words & jnp.uint32(0xFFFF0000), F32)
    return lo, hi


def _expert_up_kernel(ie_ref, in_ref, irt_ref, ivalid_ref, ifirst_ref, xs_ref, wg_ref, wu_ref,
                      bg_ref, bu_ref, h_ref, wgb_ref, wub_ref):
    w = pl.program_id(0)

    @pl.when(ifirst_ref[w] == 1)
    def _():
        wgb_ref[...] = wg_ref[0].astype(BF16)
        wub_ref[...] = wu_ref[0].astype(BF16)

    @pl.when(ivalid_ref[w] == 1)
    def _():
        rh = wgb_ref.shape[0] // 2 // LANES
        lo, hi = (a.astype(BF16) for a in _unpack_rows(_load_row_tiles(xs_ref, rh)))
        half = lo.shape[1]
        gate = _dot(lo, wgb_ref[:half, :]) + _dot(hi, wgb_ref[half:, :]) + bg_ref[0]
        up = _dot(lo, wub_ref[:half, :]) + _dot(hi, wub_ref[half:, :]) + bu_ref[0]
        gate = jnp.minimum(gate, SWIGLU_LIMIT)
        up = jnp.clip(up, -SWIGLU_LIMIT, SWIGLU_LIMIT)
        act = gate * jax.nn.sigmoid(SWIGLU_ALPHA * gate) * (up + 1.0)
        h_ref[...] = act.astype(h_ref.dtype)

    @pl.when(ivalid_ref[w] == 0)
    def _():
        h_ref[...] = jnp.zeros(h_ref.shape, h_ref.dtype)


def _expert_down_kernel(ie_ref, in_ref, irt_ref, ivalid_ref, ifirst_ref, h_ref, wd_ref, bd_ref,
                        o_ref, wdb_ref):
    w = pl.program_id(0)

    @pl.when(ifirst_ref[w] == 1)
    def _():
        wdb_ref[...] = wd_ref[0].astype(BF16)

    @pl.when(ivalid_ref[w] == 1)
    def _():
        out = _dot(h_ref[...], wdb_ref[...]) + bd_ref[0]
        half = out.shape[1] // 2
        _store_row_tiles(o_ref, _pack_rows(out[:, :half], out[:, half:]))

    @pl.when(ivalid_ref[w] == 0)
    def _():
        o_ref[...] = jnp.zeros(o_ref.shape, o_ref.dtype)


def _work_items(tiles_e, n_tiles_max, n_col):
    ne = tiles_e.shape[0]
    cum = jnp.cumsum(tiles_e)
    total_tiles = cum[-1]
    first_tile = cum - tiles_e
    item_end = cum * n_col
    w = jnp.arange(n_tiles_max * n_col, dtype=jnp.int32)
    e = jnp.minimum(jnp.sum(w[:, None] >= item_end[None, :], axis=1), ne - 1).astype(jnp.int32)
    te = jnp.maximum(tiles_e[e], 1)
    local = w - (item_end[e] - tiles_e[e] * n_col)
    valid = w < total_tiles * n_col
    spare = jnp.maximum(n_tiles_max - total_tiles, 1)
    j = w - total_tiles * n_col
    col = jnp.where(valid, local // te, j // spare)
    rt = jnp.where(valid, first_tile[e] + local % te, total_tiles + j % spare)
    first = jnp.where(valid, (local % te) == 0, False)
    i32 = lambda a: a.astype(jnp.int32)
    return i32(e), i32(col), i32(rt), i32(valid), i32(first)


def _expert_up(items, xs, w_gate, w_up, b_gate, b_up, tm, tn):
    ne, d, f = w_gate.shape
    rh = d // 2 // LANES
    n_rows = xs.shape[0] // rh
    tn = min(tn, f)
    n_items = items[0].shape[0]
    wspec = pl.BlockSpec((1, d, tn), lambda w, ie, ic, irt, iv, ifi: (ie[w], 0, ic[w]))
    bspec = pl.BlockSpec((1, 1, tn), lambda w, ie, ic, irt, iv, ifi: (ie[w], 0, ic[w]))
    grid_spec = pltpu.PrefetchScalarGridSpec(
        num_scalar_prefetch=5,
        grid=(n_items,),
        in_specs=[pl.BlockSpec((tm * rh, LANES), lambda w, ie, ic, irt, iv, ifi: (irt[w], 0)),
                  wspec, wspec, bspec, bspec],
        out_specs=pl.BlockSpec((tm, tn), lambda w, ie, ic, irt, iv, ifi: (irt[w], ic[w])),
        scratch_shapes=[pltpu.VMEM((d, tn), BF16), pltpu.VMEM((d, tn), BF16)],
    )
    return pl.pallas_call(
        _expert_up_kernel,
        grid_spec=grid_spec,
        out_shape=jax.ShapeDtypeStruct((n_rows, f), BF16),
        compiler_params=_cparams(("arbitrary",)),
    )(*items, xs, w_gate, w_up, b_gate.reshape(ne, 1, f), b_up.reshape(ne, 1, f))


def _expert_down(items, h, w_down, b_down, tm):
    n_rows, f = h.shape
    ne, _, d = w_down.shape
    rh = d // 2 // LANES
    n_items = items[0].shape[0]
    grid_spec = pltpu.PrefetchScalarGridSpec(
        num_scalar_prefetch=5,
        grid=(n_items,),
        in_specs=[pl.BlockSpec((tm, f), lambda w, ie, ic, irt, iv, ifi: (irt[w], 0)),
                  pl.BlockSpec((1, f, d), lambda w, ie, ic, irt, iv, ifi: (ie[w], 0, 0)),
                  pl.BlockSpec((1, 1, d), lambda w, ie, ic, irt, iv, ifi: (ie[w], 0, 0))],
        out_specs=pl.BlockSpec((tm * rh, LANES), lambda w, ie, ic, irt, iv, ifi: (irt[w], 0)),
        scratch_shapes=[pltpu.VMEM((f, d), BF16)],
    )
    return pl.pallas_call(
        _expert_down_kernel,
        grid_spec=grid_spec,
        out_shape=jax.ShapeDtypeStruct((n_rows * rh, LANES), jnp.uint32),
        compiler_params=_cparams(("arbitrary",)),
    )(*items, h, w_down, b_down.reshape(ne, 1, d))


def _combine_kernel(dest_ref, os_ref, tg_ref, x2_ref, g_ref, b_ref, o_ref, buf_ref, sem,
                    *, rows, alpha, rh, n_steps):
    i = pl.program_id(0)
    slot = i % 2

    def gather(step, into, first, last):
        base = step * rows * TOP_K

        def issue(r, carry):
            dst = pl.multiple_of(r * rh, rh)
            for k in range(TOP_K):
                src = pl.multiple_of(dest_ref[base + r * TOP_K + k] * rh, rh)
                pltpu.make_async_copy(os_ref.at[pl.ds(src, rh)], buf_ref.at[into, k, pl.ds(dst, rh)],
                                      sem.at[into]).start(priority=k % 2)
            return carry

        lax.fori_loop(first, last, issue, 0, unroll=ISSUE_UNROLL)

    @pl.when(i == 0)
    def _():
        gather(0, 0, 0, rows)

    @pl.when(i + 1 < n_steps)
    def _():
        gather(i + 1, 1 - slot, 0, rows // 2)

    for k in range(TOP_K):
        pltpu.make_async_copy(os_ref.at[pl.ds(0, rows * rh)], buf_ref.at[slot, k],
                              sem.at[slot]).wait()

    tg = tg_ref[...]
    lo_sum = hi_sum = None
    for k in range(TOP_K):
        lo, hi = _unpack_rows(_load_row_tiles(buf_ref.at[slot, k], rh))
        gate = tg[:, k:k + 1]
        lo_sum = gate * lo if k == 0 else lo_sum + gate * lo
        hi_sum = gate * hi if k == 0 else hi_sum + gate * hi
    y = jnp.concatenate([lo_sum, hi_sum], axis=1)
    o_ref[...] = _layer_norm(alpha * x2_ref[...] + y, g_ref[...], b_ref[...])

    @pl.when(i + 1 < n_steps)
    def _():
        gather(i + 1, 1 - slot, rows // 2, rows)


def _combine(dest, out_sorted, tg, x2, ln_g, ln_b, alpha, rows=256):
    t, d = x2.shape
    rows = min(rows, t)
    rh = d // 2 // LANES
    row = lambda a: a.reshape(1, -1).astype(F32)
    grid_spec = pltpu.PrefetchScalarGridSpec(
        num_scalar_prefetch=1,
        grid=(t // rows,),
        in_specs=[pl.BlockSpec(memory_space=pl.ANY),
                  pl.BlockSpec((rows, LANES), lambda i, s: (i, 0)),
                  pl.BlockSpec((rows, d), lambda i, s: (i, 0)),
                  pl.BlockSpec((1, d), lambda i, s: (0, 0)),
                  pl.BlockSpec((1, d), lambda i, s: (0, 0))],
        out_specs=pl.BlockSpec((rows, d), lambda i, s: (i, 0)),
        scratch_shapes=[pltpu.VMEM((2, TOP_K, rows * rh, LANES), jnp.uint32),
                        pltpu.SemaphoreType.DMA((2,))],
    )
    return pl.pallas_call(
        functools.partial(_combine_kernel, rows=rows, alpha=alpha, rh=rh, n_steps=t // rows),
        grid_spec=grid_spec,
        out_shape=jax.ShapeDtypeStruct((t, d), F32),
        compiler_params=_cparams(("arbitrary",)),
    )(dest, out_sorted, tg, x2, row(ln_g), row(ln_b))


def _moe(x2, x2p, te, tg, rk, counts, w_gate, b_gate, w_up, b_up, w_down, b_down, ln_g, ln_b,
         alpha, tm=512, tn_up=1024):
    t, d = x2.shape
    ne = w_gate.shape[0]
    n_assign = t * TOP_K
    tm = min(tm, t)
    n_tiles_max = -(-n_assign // tm) + ne
    n_rows = n_tiles_max * tm

    counts = counts[0, :ne].astype(jnp.int32)
    tiles_e = (counts + tm - 1) // tm
    pad_start = (jnp.cumsum(tiles_e) - tiles_e) * tm
    hit = te[:, :TOP_K, None] == jnp.arange(ne, dtype=jnp.int32)
    dest = (jnp.sum(jnp.where(hit, pad_start, 0), axis=-1) + rk[:, :TOP_K]).reshape(n_assign)

    cum_tiles = jnp.cumsum(tiles_e)
    last_tile = jnp.where(tiles_e > 0, cum_tiles - 1, -1)
    tail = cum_tiles[-1] + jnp.arange(ne, dtype=jnp.int32)
    fill_tiles = jnp.concatenate([last_tile, jnp.where(tail < n_tiles_max, tail, -1)])

    xs = _dispatch(dest, fill_tiles.astype(jnp.int32), x2p, n_rows, tm, d // 2 // LANES)
    f = w_gate.shape[2]
    tn_up = min(tn_up, f)
    items_up = _work_items(tiles_e, n_tiles_max, f // tn_up)
    h = _expert_up(items_up, xs, w_gate, w_up, b_gate, b_up, tm, tn_up)
    items_dn = _work_items(tiles_e, n_tiles_max, 1)
    out_sorted = _expert_down(items_dn, h, w_down, b_down, tm)
    return _combine(dest, out_sorted, tg, x2, ln_g, ln_b, alpha)


def kernel(x, mem, w_in, conv_w, attn_subln_w, lambda_q1, lambda_k1, lambda_q2, lambda_k2, w_out, ln1_g, ln1_b, mem_wq, mem_wkv, mem_wo, ln2_g, ln2_b, router_w, router_b, w_gate, b_gate, w_up, b_up, w_down, b_down, ln3_g, ln3_b):
    bsz, seq, d = x.shape
    n_mem = mem.shape[1]
    depth = w_in.shape[0]
    alpha = (2.0 * depth) ** 0.25
    slopes = jnp.exp2(-8.0 / N_ATTN_HEADS * jnp.arange(1, N_ATTN_HEADS + 1, dtype=F32))
    xf = x.reshape(bsz * seq, d)
    memf = mem.reshape(bsz * n_mem, d)
    for l in range(depth):
        lambda_init = 0.8 - 0.6 * math.exp(-0.3 * l)
        proj = _projection(xf, w_in[l].astype(BF16), 1024, 1024)
        oattn = _diff_attention(proj.reshape(bsz, seq, -1), slopes, lambda_q1[l], lambda_k1[l],
                                lambda_q2[l], lambda_k2[l], attn_subln_w[l], lambda_init)
        x1, x1b = _out_projection(oattn.reshape(bsz * seq, -1), proj, xf, conv_w[l], w_out[l],
                                  ln1_g[l], ln1_b[l], seq, alpha)
        kv = _projection(memf, mem_wkv[l].astype(BF16), 1024, 1024)
        x2, x2p, te, tg, rk, counts = _memory_attention(
            x1b, x1, kv, mem_wq[l], mem_wo[l], ln2_g[l], ln2_b[l], router_w[l], router_b[l],
            seq, n_mem, alpha)
        xf = _moe(x2, x2p, te, tg, rk, counts, w_gate[l], b_gate[l], w_up[l], b_up[l],
                  w_down[l], b_down[l], ln3_g[l], ln3_b[l], alpha)
    return xf.reshape(bsz, seq, d)
```
